```python
import jax, jax.numpy as jnp
from jax import lax
import numpy as np

D_MODEL = 2048
BATCH = 4
SEQ = 4096
DEPTH = 2

CHUNK = 64
Q_BLOCK = 128
NEG_INF = -1e30
NORM_EPS = 1e-6

N_HEADS_A = 8
QK_NOPE = 128
QK_ROPE = 64
V_HEAD_A = 128
Q_LORA = 768
KV_LORA = 256
ROPE_THETA = 10000.0

N_HEADS_B = 8
HEAD_B = 128
LEFT_CHUNKS = 8
BAND = (LEFT_CHUNKS + 1) * CHUNK
MAX_LEFT = 128
REL_SIZE = MAX_LEFT + CHUNK

N_HEADS_C = 8
HEAD_C = 128
FORGET_BIAS_INIT = 3.0

N_BRANCH = 3
BRANCH_WIDTH = N_HEADS_A * V_HEAD_A

B_QKV = 3 * N_HEADS_B * HEAD_B
C_QKV = 3 * N_HEADS_C * HEAD_C
IN_SPLITS = [Q_LORA, Q_LORA + KV_LORA, Q_LORA + KV_LORA + QK_ROPE,
             Q_LORA + KV_LORA + QK_ROPE + B_QKV,
             Q_LORA + KV_LORA + QK_ROPE + B_QKV + C_QKV]
D_IN = Q_LORA + KV_LORA + QK_ROPE + B_QKV + C_QKV + N_HEADS_C

N_EXPERTS = 32
TOP_K = 4
D_FF = D_MODEL
SWIGLU_ALPHA = 1.702
SWIGLU_LIMIT = 7.0
MOE_BLOCK = 256

kernel_name = "hybrid_chunk_causal_mla_band_fox_moe"


def _rmsnorm(x, g):
    xf = x.astype(jnp.float32)
    y = xf * lax.rsqrt(jnp.mean(xf * xf, axis=-1, keepdims=True) + NORM_EPS)
    return (y * g.astype(jnp.float32)).astype(x.dtype)


def _rope_tables(positions, dtype):
    inv = ROPE_THETA ** (-jnp.arange(0, QK_ROPE, 2, dtype=jnp.float32) / QK_ROPE)
    ang = positions.astype(jnp.float32)[..., None] * inv
    return jnp.cos(ang).astype(dtype), jnp.sin(ang).astype(dtype)


def _rope(x, cos, sin):
    half = x.shape[-1] // 2
    x1, x2 = x[..., :half], x[..., half:]
    return jnp.concatenate([x1 * cos - x2 * sin, x2 * cos + x1 * sin], axis=-1)


def _to_blocks(t, size):
    b, s = t.shape[:2]
    return jnp.swapaxes(t.reshape(b, s // size, size, *t.shape[2:]), 0, 1)


def _from_blocks(o):
    o = jnp.swapaxes(o, 0, 1)
    return o.reshape(o.shape[0], o.shape[1] * o.shape[2], -1)


def _mla_attention(q_nope, q_rope, k_nope, k_rope, v):
    s = q_nope.shape[1]
    scale = (QK_NOPE + QK_ROPE) ** -0.5
    key_chunk = jnp.arange(s) // CHUNK

    def step(args):
        qn, qr, i = args
        logits = (jnp.einsum('bqhd,bkhd->bhqk', qn, k_nope)
                  + jnp.einsum('bqhr,bkr->bhqk', qr, k_rope)).astype(jnp.float32) * scale
        q_chunk = (i * Q_BLOCK + jnp.arange(Q_BLOCK)) // CHUNK
        mask = key_chunk[None, :] <= q_chunk[:, None]
        p = jax.nn.softmax(jnp.where(mask, logits, NEG_INF), axis=-1).astype(v.dtype)
        return jnp.einsum('bhqk,bkhd->bqhd', p, v)

    nqb = s // Q_BLOCK
    o = lax.map(step, (_to_blocks(q_nope, Q_BLOCK), _to_blocks(q_rope, Q_BLOCK), jnp.arange(nqb)))
    return _from_blocks(o)


def _chunk_band_attention(q, k, v, rel_bias):
    b, s, h, d = q.shape
    nc = s // CHUNK
    left = LEFT_CHUNKS * CHUNK
    scale = d ** -0.5
    k_pad = jnp.pad(k, ((0, 0), (left, 0), (0, 0), (0, 0)))
    v_pad = jnp.pad(v, ((0, 0), (left, 0), (0, 0), (0, 0)))
    rel = jnp.arange(BAND)[None, :] - left - jnp.arange(CHUNK)[:, None]
    bias = rel_bias[:, jnp.clip(rel, -MAX_LEFT, CHUNK - 1) + MAX_LEFT].astype(jnp.float32)

    def step(args):
        qn, n = args
        kb = lax.dynamic_slice_in_dim(k_pad, n * CHUNK, BAND, axis=1)
        vb = lax.dynamic_slice_in_dim(v_pad, n * CHUNK, BAND, axis=1)
        logits = jnp.einsum('bqhd,bkhd->bhqk', qn, kb).astype(jnp.float32) * scale + bias
        valid = (n * CHUNK - left + jnp.arange(BAND)) >= 0
        p = jax.nn.softmax(jnp.where(valid, logits, NEG_INF), axis=-1).astype(vb.dtype)
        return jnp.einsum('bhqk,bkhd->bqhd', p, vb)

    o = lax.map(step, (_to_blocks(q, CHUNK), jnp.arange(nc)))
    return _from_blocks(o)


def _forgetting_attention(q, k, v, log_f):
    s, d = q.shape[1], q.shape[-1]
    scale = d ** -0.5
    fcum = jnp.cumsum(log_f, axis=1)
    fk = jnp.swapaxes(fcum, 1, 2)
    kpos = jnp.arange(s)

    def step(args):
        qi, fq, i = args
        logits = (jnp.einsum('bqhd,bkhd->bhqk', qi, k).astype(jnp.float32) * scale
                  + jnp.swapaxes(fq, 1, 2)[..., None] - fk[:, :, None, :])
        qpos = i * Q_BLOCK + jnp.arange(Q_BLOCK)
        mask = kpos[None, :] <= qpos[:, None]
        p = jax.nn.softmax(jnp.where(mask, logits, NEG_INF), axis=-1).astype(v.dtype)
        return jnp.einsum('bhqk,bkhd->bqhd', p, v)

    nqb = s // Q_BLOCK
    o = lax.map(step, (_to_blocks(q, Q_BLOCK), _to_blocks(fcum, Q_BLOCK), jnp.arange(nqb)))
    return _from_blocks(o)


def _mixer(u, cos, sin, w_in, g_q_lat, w_uq, g_kv_lat, w_ukv, rel_bias, b_forget,
           w_gate, b_gate, w_branch, w_out):
    b, s, d = u.shape
    proj = u @ w_in
    q_lat, kv_lat, k_rope, qkv_b, qkv_c, f_logit = jnp.split(proj, IN_SPLITS, axis=-1)

    q_a = (_rmsnorm(q_lat, g_q_lat) @ w_uq).reshape(b, s, N_HEADS_A, QK_NOPE + QK_ROPE)
    q_nope = q_a[..., :QK_NOPE]
    q_rope = _rope(q_a[..., QK_NOPE:], cos[:, :, None, :], sin[:, :, None, :])
    kv = (_rmsnorm(kv_lat, g_kv_lat) @ w_ukv).reshape(b, s, N_HEADS_A, QK_NOPE + V_HEAD_A)
    k_nope, v_a = kv[..., :QK_NOPE], kv[..., QK_NOPE:]
    k_rope = _rope(k_rope, cos, sin)
    o_a = _mla_attention(q_nope, q_rope, k_nope, k_rope, v_a)

    qkv_b = qkv_b.reshape(b, s, 3, N_HEADS_B, HEAD_B)
    o_b = _chunk_band_attention(qkv_b[:, :, 0], qkv_b[:, :, 1], qkv_b[:, :, 2], rel_bias)

    qkv_c = qkv_c.reshape(b, s, 3, N_HEADS_C, HEAD_C)
    log_f = jax.nn.log_sigmoid(f_logit.astype(jnp.float32) + b_forget.astype(jnp.float32))
    o_c = _forgetting_attention(qkv_c[:, :, 0], qkv_c[:, :, 1], qkv_c[:, :, 2], log_f)

    merged = None
    for g, o in enumerate((o_a, o_b, o_c)):
        gate = jax.nn.sigmoid(u @ w_gate[:, g * d:(g + 1) * d] + b_gate[g * d:(g + 1) * d])
        term = gate * (o @ w_branch[g])
        merged = term if merged is None else merged + term
    return merged @ w_out


def _moe(h, w_router, b_router, w_mlp1, b_mlp1, w_mlp2, b_mlp2):
    b, s, d = h.shape
    t = b * s
    xt = h.reshape(t, d)
    logits = (xt @ w_router).astype(jnp.float32) + b_router.astype(jnp.float32)
    top_val, top_idx = lax.top_k(logits, TOP_K)
    top_w = jax.nn.softmax(top_val, axis=-1)
    flat_e = top_idx.reshape(-1)
    order = jnp.argsort(flat_e)
    sorted_e = flat_e[order]
    sorted_tok = order // TOP_K
    sorted_w = top_w.reshape(-1)[order].astype(h.dtype)
    counts = jnp.bincount(flat_e, length=N_EXPERTS)
    padded = (counts + MOE_BLOCK - 1) // MOE_BLOCK * MOE_BLOCK
    padded_end = jnp.cumsum(padded)
    start = jnp.cumsum(counts) - counts
    dest = (padded_end - padded)[sorted_e] + jnp.arange(t * TOP_K) - start[sorted_e]
    n_rows = t * TOP_K + N_EXPERTS * MOE_BLOCK
    n_blocks = n_rows // MOE_BLOCK
    block_e = jnp.minimum(jnp.searchsorted(padded_end, jnp.arange(n_blocks) * MOE_BLOCK, side='right'),
                          N_EXPERTS - 1)
    xs = jnp.zeros((n_rows, d), h.dtype).at[dest].set(xt[sorted_tok])

    def expert_block(args):
        xb, e = args
        hh = xb @ w_mlp1[e] + b_mlp1[e]
        glu = jnp.minimum(hh[:, :D_FF], SWIGLU_LIMIT)
        lin = jnp.clip(hh[:, D_FF:], -SWIGLU_LIMIT, SWIGLU_LIMIT)
        act = glu * jax.nn.sigmoid(SWIGLU_ALPHA * glu) * (lin + 1)
        return act @ w_mlp2[e] + b_mlp2[e]

    ys = lax.map(expert_block, (xs.reshape(n_blocks, MOE_BLOCK, d), block_e)).reshape(n_rows, d)
    out = jnp.zeros((t, d), h.dtype).at[sorted_tok].add(ys[dest] * sorted_w[:, None])
    return out.reshape(b, s, d)


def setup_inputs(seed: int = 0) -> dict:
    key = jax.random.key(seed)
    ks = jax.random.split(key, 32)
    L, D = DEPTH, D_MODEL
    f32 = jnp.float32

    def nrm(k, shape, fan_in, gain=1.0):
        return (gain * fan_in ** -0.5) * jax.random.normal(k, shape, f32)

    def small(k, shape, s=0.02):
        return s * jax.random.normal(k, shape, f32)

    def gain(k, shape):
        return 1.0 + 0.05 * jax.random.normal(k, shape, f32)

    x = jax.random.normal(ks[0], (BATCH, SEQ, D), f32)
    c = jax.random.normal(ks[1], (BATCH, D), f32)
    positions = (jax.random.randint(ks[2], (BATCH, 1), 0, 4096, dtype=jnp.int32)
                 + jnp.arange(SEQ, dtype=jnp.int32)[None, :])
    return {
        'x': x,
        'c': c,
        'positions': positions,
        'w_mod': nrm(ks[3], (L, D, 6 * D), D, 0.5),
        'b_mod': small(ks[4], (L, 6 * D)),
        'g_norm1': gain(ks[5], (L, D)),
        'g_norm2': gain(ks[6], (L, D)),
        'w_in': nrm(ks[7], (L, D, D_IN), D),
        'g_q_lat': gain(ks[8], (L, Q_LORA)),
        'w_uq': nrm(ks[9], (L, Q_LORA, N_HEADS_A * (QK_NOPE + QK_ROPE)), Q_LORA),
        'g_kv_lat': gain(ks[10], (L, KV_LORA)),
        'w_ukv': nrm(ks[11], (L, KV_LORA, N_HEADS_A * (QK_NOPE + V_HEAD_A)), KV_LORA),
        'rel_bias': small(ks[12], (L, N_HEADS_B, REL_SIZE), 0.5),
        'b_forget': FORGET_BIAS_INIT + 0.5 * jax.random.normal(ks[13], (L, N_HEADS_C), f32),
        'w_gate': nrm(ks[14], (L, D, N_BRANCH * D), D),
        'b_gate': small(ks[15], (L, N_BRANCH * D)),
        'w_branch': nrm(ks[16], (L, N_BRANCH, BRANCH_WIDTH, D), BRANCH_WIDTH),
        'w_out': nrm(ks[17], (L, D, D), D),
        'w_router': nrm(ks[18], (L, D, N_EXPERTS), D),
        'b_router': small(ks[19], (L, N_EXPERTS), 0.01),
        'w_mlp1': nrm(ks[20], (L, N_EXPERTS, D, 2 * D_FF), D),
        'b_mlp1': small(ks[21], (L, N_EXPERTS, 2 * D_FF)),
        'w_mlp2': nrm(ks[22], (L, N_EXPERTS, D_FF, D), D_FF),
        'b_mlp2': small(ks[23], (L, N_EXPERTS, D)),
        'g_final': gain(ks[24], (D,)),
    }


def reference(x, c, positions, w_mod, b_mod, g_norm1, g_norm2, w_in, g_q_lat, w_uq, g_kv_lat,
              w_ukv, rel_bias, b_forget, w_gate, b_gate, w_branch, w_out, w_router, b_router,
              w_mlp1, b_mlp1, w_mlp2, b_mlp2, g_final):
    cos, sin = _rope_tables(positions, x.dtype)
    c_act = jax.nn.silu(c)
    h = x
    for l in range(DEPTH):
        mod = c_act @ w_mod[l] + b_mod[l]
        sh1, sc1, gt1, sh2, sc2, gt2 = [m[:, None, :] for m in jnp.split(mod, 6, axis=-1)]
        u = _rmsnorm(h, g_norm1[l]) * (1 + sc1) + sh1
        h = h + gt1 * _mixer(u, cos, sin, w_in[l], g_q_lat[l], w_uq[l], g_kv_lat[l], w_ukv[l],
                             rel_bias[l], b_forget[l], w_gate[l], b_gate[l], w_branch[l], w_out[l])
        u = _rmsnorm(h, g_norm2[l]) * (1 + sc2) + sh2
        h = h + gt2 * _moe(u, w_router[l], b_router[l], w_mlp1[l], b_mlp1[l], w_mlp2[l], b_mlp2[l])
    return _rmsnorm(h, g_final)
```

```python
import functools

import jax
import jax.numpy as jnp
from jax import lax
from jax.experimental import pallas as pl
from jax.experimental.pallas import tpu as pltpu

CHUNK = 64
QK_NOPE = 128
QK_ROPE = 64
HEAD = 128
LEFT_CHUNKS = 8
MAX_LEFT = 128
ROPE_THETA = 10000.0
NEG_INF = -1e30
NORM_EPS = 1e-6
TOP_K = 4
SWIGLU_ALPHA = 1.702
SWIGLU_LIMIT = 7.0

LANES = 128
MLA_QK = 2 * LANES
VMEM_LIMIT = 56 * 1024 * 1024
MOE_ROWS = 256

F32 = jnp.float32
BF16 = jnp.bfloat16


def _tile(n, pref):
    t = min(n, pref)
    while n % t:
        t //= 2
    return t


def _params(*sem):
    return pltpu.CompilerParams(dimension_semantics=sem, vmem_limit_bytes=VMEM_LIMIT)


def _dot(a, b):
    return jnp.dot(a, b, preferred_element_type=F32)


def _dot_nt(a, b):
    return lax.dot_general(a, b, (((1,), (1,)), ((), ())), preferred_element_type=F32)


def _sigmoid(x):
    return 1.0 / (1.0 + jnp.exp(-x))


def _mod_kernel(c_ref, w_ref, b_ref, o_ref):
    c = c_ref[...]
    ca = (c * _sigmoid(c)).astype(BF16)
    o_ref[0] = _dot(ca, w_ref[0].astype(BF16)) + b_ref[0]


def _modulation(c, w_mod, b_mod):
    nl, d, n = w_mod.shape
    b = c.shape[0]
    bp = 16
    cp = jnp.zeros((bp, d), F32).at[:b].set(c)
    tn = _tile(n, 1024)
    out = pl.pallas_call(
        _mod_kernel,
        grid=(nl, n // tn),
        in_specs=[pl.BlockSpec((bp, d), lambda l, j: (0, 0)),
                  pl.BlockSpec((1, d, tn), lambda l, j: (l, 0, j)),
                  pl.BlockSpec((1, 1, tn), lambda l, j: (l, 0, j))],
        out_specs=pl.BlockSpec((1, bp, tn), lambda l, j: (l, 0, j)),
        out_shape=jax.ShapeDtypeStruct((nl, bp, n), F32),
        compiler_params=_params("arbitrary", "arbitrary"),
    )(cp, w_mod, b_mod.reshape(nl, 1, n))
    return out[:, :b].reshape(nl, b, 6, d)


def _rms(x):
    return x * lax.rsqrt(jnp.mean(x * x, axis=-1, keepdims=True) + NORM_EPS)


def _norm_mod_kernel(h_ref, mod_ref, g_ref, u_ref, *, row):
    y = _rms(h_ref[0]) * g_ref[...]
    u = y * (1.0 + mod_ref[0, row + 1:row + 2, :]) + mod_ref[0, row:row + 1, :]
    u_ref[0] = u.astype(u_ref.dtype)


def _norm_mod(h, mod, g, row):
    b, s, d = h.shape
    ts = _tile(s, 512)
    return pl.pallas_call(
        functools.partial(_norm_mod_kernel, row=row),
        grid=(b, s // ts),
        in_specs=[pl.BlockSpec((1, ts, d), lambda i, j: (i, j, 0)),
                  pl.BlockSpec((1, 6, d), lambda i, j: (i, 0, 0)),
                  pl.BlockSpec((1, d), lambda i, j: (0, 0))],
        out_specs=pl.BlockSpec((1, ts, d), lambda i, j: (i, j, 0)),
        out_shape=jax.ShapeDtypeStruct((b, s, d), BF16),
        compiler_params=_params("arbitrary", "arbitrary"),
    )(h, mod, g.reshape(1, d))


def _final_norm_kernel(h_ref, g_ref, o_ref):
    o_ref[0] = _rms(h_ref[0]) * g_ref[...]


def _final_norm(h, g):
    b, s, d = h.shape
    ts = _tile(s, 512)
    return pl.pallas_call(
        _final_norm_kernel,
        grid=(b, s // ts),
        in_specs=[pl.BlockSpec((1, ts, d), lambda i, j: (i, j, 0)),
                  pl.BlockSpec((1, d), lambda i, j: (0, 0))],
        out_specs=pl.BlockSpec((1, ts, d), lambda i, j: (i, j, 0)),
        out_shape=jax.ShapeDtypeStruct((b, s, d), F32),
        compiler_params=_params("arbitrary", "arbitrary"),
    )(h, g.reshape(1, d))


def _mm_kernel(x_ref, w_ref, b_ref, s_ref, o_ref, *, act):
    acc = (_dot(x_ref[...], w_ref[...]) + b_ref[...]) * s_ref[...]
    if act == "sigmoid":
        acc = _sigmoid(acc)
    o_ref[...] = acc.astype(o_ref.dtype)


def _mm(x, w, bias, scale, out_dtype, act=None):
    m, k = x.shape
    n = w.shape[1]
    tm = _tile(m, 1024)
    tn = n if n <= 1536 else _tile(n, 1024)
    return pl.pallas_call(
        functools.partial(_mm_kernel, act=act),
        grid=(n // tn, m // tm),
        in_specs=[pl.BlockSpec((tm, k), lambda j, i: (i, 0)),
                  pl.BlockSpec((k, tn), lambda j, i: (0, j)),
                  pl.BlockSpec((1, tn), lambda j, i: (0, j)),
                  pl.BlockSpec((1, tn), lambda j, i: (0, j))],
        out_specs=pl.BlockSpec((tm, tn), lambda j, i: (i, j)),
        out_shape=jax.ShapeDtypeStruct((m, n), out_dtype),
        compiler_params=_params("arbitrary", "arbitrary"),
    )(x, w, bias.reshape(1, n), scale.reshape(1, n))


def _mla_prep_kernel(lat_ref, c_ref, s_ref, wq_ref, wqr_ref, wkv_ref, gq_ref, gkv_ref,
                     q_ref, k_ref, v_ref, *, q_lora, kv_lora, n_heads, scale):
    cos = c_ref[...]
    sin = s_ref[...]
    qn = (_rms(lat_ref[:, :q_lora]) * gq_ref[...]).astype(BF16)
    qa = _dot(qn, wq_ref[...])
    qr = _dot(qn, wqr_ref[...])
    kvn = (_rms(lat_ref[:, q_lora:q_lora + kv_lora]) * gkv_ref[...]).astype(BF16)
    kv = _dot(kvn, wkv_ref[...])
    o = q_lora + kv_lora
    k_rope = (lat_ref[:, o:o + LANES] * cos + lat_ref[:, o + LANES:o + 2 * LANES] * sin).astype(BF16)
    for h in range(n_heads):
        a = h * MLA_QK
        q_ref[:, a:a + LANES] = (qa[:, a:a + LANES] * scale).astype(BF16)
        rope = qa[:, a + LANES:a + MLA_QK] * cos + qr[:, h * LANES:(h + 1) * LANES] * sin
        q_ref[:, a + LANES:a + MLA_QK] = (rope * scale).astype(BF16)
        k_ref[:, a:a + LANES] = kv[:, h * LANES:(h + 1) * LANES].astype(BF16)
        k_ref[:, a + LANES:a + MLA_QK] = k_rope
    v_ref[...] = kv[:, n_heads * LANES:].astype(BF16)


def _mla_prep(lat, ctab, stab, wq, wqr, wkv, gq, gkv, n_heads):
    t, nl = lat.shape
    q_lora, kv_lora = gq.shape[0], gkv.shape[0]
    tm = _tile(t, 512)
    kern = functools.partial(_mla_prep_kernel, q_lora=q_lora, kv_lora=kv_lora, n_heads=n_heads,
                             scale=float(QK_NOPE + QK_ROPE) ** -0.5)
    full = lambda a: pl.BlockSpec(a.shape, lambda i: (0,) * a.ndim)
    gq2, gkv2 = gq.reshape(1, -1), gkv.reshape(1, -1)
    return pl.pallas_call(
        kern,
        grid=(t // tm,),
        in_specs=[pl.BlockSpec((tm, nl), lambda i: (i, 0)),
                  pl.BlockSpec((tm, LANES), lambda i: (i, 0)),
                  pl.BlockSpec((tm, LANES), lambda i: (i, 0)),
                  full(wq), full(wqr), full(wkv), full(gq2), full(gkv2)],
        out_specs=[pl.BlockSpec((tm, n_heads * MLA_QK), lambda i: (i, 0)),
                   pl.BlockSpec((tm, n_heads * MLA_QK), lambda i: (i, 0)),
                   pl.BlockSpec((tm, n_heads * HEAD), lambda i: (i, 0))],
        out_shape=[jax.ShapeDtypeStruct((t, n_heads * MLA_QK), BF16),
                   jax.ShapeDtypeStruct((t, n_heads * MLA_QK), BF16),
                   jax.ShapeDtypeStruct((t, n_heads * HEAD), BF16)],
        compiler_params=_params("arbitrary"),
    )(lat, ctab, stab, wq, wqr, wkv, gq2, gkv2)


def _fgate_kernel(f_ref, b_ref, o_ref):
    x = f_ref[...] + b_ref[...]
    v = jnp.minimum(x, 0.0) - jnp.log(1.0 + jnp.exp(-jnp.abs(x)))
    n = v.shape[1]
    lane = lax.broadcasted_iota(jnp.int32, v.shape, 1)
    sh = 1
    while sh < n:
        v = v + jnp.where(lane >= sh, pltpu.roll(v, sh, 1), 0.0)
        sh *= 2
    o_ref[...] = v


def _fgate(f_t, bias):
    r, s = f_t.shape
    return pl.pallas_call(
        _fgate_kernel,
        grid=(1,),
        in_specs=[pl.BlockSpec((r, s), lambda i: (0, 0)), pl.BlockSpec((r, 1), lambda i: (0, 0))],
        out_specs=pl.BlockSpec((r, s), lambda i: (0, 0)),
        out_shape=jax.ShapeDtypeStruct((r, s), F32),
        compiler_params=_params("arbitrary"),
    )(f_t, bias)


def _flash_init(m_ref, l_ref, acc_ref):
    m_ref[...] = jnp.full(m_ref.shape, NEG_INF, F32)
    l_ref[...] = jnp.zeros(l_ref.shape, F32)
    acc_ref[...] = jnp.zeros(acc_ref.shape, F32)


def _flash_update(s, v, m_ref, l_ref, acc_ref):
    m_prev = m_ref[...]
    m_new = jnp.maximum(m_prev, jnp.max(s, axis=-1, keepdims=True))
    alpha = jnp.exp(m_prev - m_new)
    p = jnp.exp(s - m_new)
    l_ref[...] = alpha * l_ref[...] + jnp.sum(p, axis=-1, keepdims=True)
    acc_ref[...] = alpha * acc_ref[...] + _dot(p.astype(BF16), v)
    m_ref[...] = m_new


def _flash_finish(o_ref, l_ref, acc_ref):
    o_ref[0] = (acc_ref[...] / l_ref[...]).astype(o_ref.dtype)


def _causal_attn_kernel(*refs, t, mask_unit, has_decay):
    if has_decay:
        q_ref, k_ref, v_ref, f_ref, o_ref, m_ref, l_ref, acc_ref = refs
    else:
        q_ref, k_ref, v_ref, o_ref, m_ref, l_ref, acc_ref = refs
    i = pl.program_id(2)
    q = q_ref[0]
    _flash_init(m_ref, l_ref, acc_ref)

    def scores(j):
        start = pl.multiple_of(j * t, t)
        s = _dot_nt(q, k_ref[0, pl.ds(start, t), :])
        if has_decay:
            s = s - f_ref[0, :, pl.ds(start, t)]
        return s, v_ref[0, pl.ds(start, t), :]

    def body(j, carry):
        s, v = scores(j)
        _flash_update(s, v, m_ref, l_ref, acc_ref)
        return carry

    lax.fori_loop(0, i, body, 0)
    s, v = scores(i)
    row = lax.broadcasted_iota(jnp.int32, (t, t), 0)
    col = lax.broadcasted_iota(jnp.int32, (t, t), 1)
    if mask_unit > 1:
        row, col = row // mask_unit, col // mask_unit
    _flash_update(jnp.where(col <= row, s, NEG_INF), v, m_ref, l_ref, acc_ref)
    _flash_finish(o_ref, l_ref, acc_ref)


def _causal_attn(q_arr, k_arr, v_arr, q_off, k_off, v_off, n_heads, dk, mask_unit, fcum=None):
    b, s, _ = q_arr.shape
    t = _tile(s, 512)
    has_decay = fcum is not None
    in_specs = [pl.BlockSpec((1, t, dk), lambda bi, h, i: (bi, i, q_off + h)),
                pl.BlockSpec((1, s, dk), lambda bi, h, i: (bi, 0, k_off + h)),
                pl.BlockSpec((1, s, HEAD), lambda bi, h, i: (bi, 0, v_off + h))]
    args = [q_arr, k_arr, v_arr]
    if has_decay:
        in_specs.append(pl.BlockSpec((1, 1, s), lambda bi, h, i: (bi * n_heads + h, 0, 0)))
        args.append(fcum.reshape(b * n_heads, 1, s))
    return pl.pallas_call(
        functools.partial(_causal_attn_kernel, t=t, mask_unit=mask_unit, has_decay=has_decay),
        grid=(b, n_heads, s // t),
        in_specs=in_specs,
        out_specs=pl.BlockSpec((1, t, HEAD), lambda bi, h, i: (bi, i, h)),
        out_shape=jax.ShapeDtypeStruct((b, s, n_heads * HEAD), BF16),
        scratch_shapes=[pltpu.VMEM((t, 1), F32), pltpu.VMEM((t, 1), F32), pltpu.VMEM((t, HEAD), F32)],
        compiler_params=_params("arbitrary", "arbitrary", "arbitrary"),
    )(*args)


def _band_attn_kernel(q_ref, k_ref, v_ref, bias_ref, o_ref, m_ref, l_ref, acc_ref, *, t, n_off):
    i = pl.program_id(2)
    q = q_ref[0]
    _flash_init(m_ref, l_ref, acc_ref)
    for o in range(n_off):
        kb = i - (n_off - 1) + o

        @pl.when(kb >= 0)
        def _():
            start = pl.multiple_of(kb * t, t)
            s = _dot_nt(q, k_ref[0, pl.ds(start, t), :]) + bias_ref[0, o]
            _flash_update(s, v_ref[0, pl.ds(start, t), :], m_ref, l_ref, acc_ref)

    _flash_finish(o_ref, l_ref, acc_ref)


def _band_bias(rel_bias, t, n_off):
    qpos = jnp.arange(t)[:, None]
    tabs = []
    for o in range(n_off):
        kpos = jnp.arange(t)[None, :] + (o - (n_off - 1)) * t
        rel = kpos - qpos
        dchunk = kpos // CHUNK - qpos // CHUNK
        bias = rel_bias[:, jnp.clip(rel, -MAX_LEFT, CHUNK - 1) + MAX_LEFT].astype(F32)
        tabs.append(jnp.where((dchunk <= 0) & (dchunk >= -LEFT_CHUNKS), bias, NEG_INF))
    return jnp.stack(tabs, axis=1)


def _band_attn(qkv, rel_bias, n_heads):
    b, s, _ = qkv.shape
    t = _tile(s, 256)
    n_off = (LEFT_CHUNKS * CHUNK + t - 1) // t + 1
    bias = _band_bias(rel_bias, t, n_off)
    return pl.pallas_call(
        functools.partial(_band_attn_kernel, t=t, n_off=n_off),
        grid=(b, n_heads, s // t),
        in_specs=[pl.BlockSpec((1, t, HEAD), lambda bi, h, i: (bi, i, h)),
                  pl.BlockSpec((1, s, HEAD), lambda bi, h, i: (bi, 0, n_heads + h)),
                  pl.BlockSpec((1, s, HEAD), lambda bi, h, i: (bi, 0, 2 * n_heads + h)),
                  pl.BlockSpec((1, n_off, t, t), lambda bi, h, i: (h, 0, 0, 0))],
        out_specs=pl.BlockSpec((1, t, HEAD), lambda bi, h, i: (bi, i, h)),
        out_shape=jax.ShapeDtypeStruct((b, s, n_heads * HEAD), BF16),
        scratch_shapes=[pltpu.VMEM((t, 1), F32), pltpu.VMEM((t, 1), F32), pltpu.VMEM((t, HEAD), F32)],
        compiler_params=_params("arbitrary", "arbitrary", "arbitrary"),
    )(qkv, qkv, qkv, bias)


def _merge_kernel(oa_ref, ob_ref, oc_ref, ga_ref, gb_ref, gc_ref, wa_ref, wb_ref, wc_ref, o_ref):
    acc = ga_ref[...].astype(F32) * _dot(oa_ref[...], wa_ref[0])
    acc += gb_ref[...].astype(F32) * _dot(ob_ref[...], wb_ref[0])
    acc += gc_ref[...].astype(F32) * _dot(oc_ref[...], wc_ref[0])
    o_ref[...] = acc.astype(o_ref.dtype)


def _merge(oa, ob, oc, gates, w_branch):
    t, bw = oa.shape
    d = w_branch.shape[2]
    tm, tn = _tile(t, 512), _tile(d, 1024)
    nj = d // tn
    o_spec = pl.BlockSpec((tm, bw), lambda j, i: (i, 0))
    g_spec = lambda g: pl.BlockSpec((tm, tn), lambda j, i: (i, g * nj + j))
    w_spec = lambda g: pl.BlockSpec((1, bw, tn), lambda j, i: (g, 0, j))
    return pl.pallas_call(
        _merge_kernel,
        grid=(nj, t // tm),
        in_specs=[o_spec, o_spec, o_spec, g_spec(0), g_spec(1), g_spec(2), w_spec(0), w_spec(1), w_spec(2)],
        out_specs=pl.BlockSpec((tm, tn), lambda j, i: (i, j)),
        out_shape=jax.ShapeDtypeStruct((t, d), BF16),
        compiler_params=_params("arbitrary", "arbitrary"),
    )(oa, ob, oc, gates, gates, gates, w_branch, w_branch, w_branch)


def _out_proj_kernel(x_ref, w_ref, h_ref, mod_ref, o_ref, *, row):
    o_ref[0] = h_ref[0] + mod_ref[0, row:row + 1, :] * _dot(x_ref[0], w_ref[...])


def _out_proj(x, w, h, mod, row):
    b, s, k = x.shape
    d = w.shape[1]
    tm, tn = _tile(s, 1024), _tile(d, 1024)
    return pl.pallas_call(
        functools.partial(_out_proj_kernel, row=row),
        grid=(d // tn, b, s // tm),
        in_specs=[pl.BlockSpec((1, tm, k), lambda j, bi, i: (bi, i, 0)),
                  pl.BlockSpec((k, tn), lambda j, bi, i: (0, j)),
                  pl.BlockSpec((1, tm, tn), lambda j, bi, i: (bi, i, j)),
                  pl.BlockSpec((1, 6, tn), lambda j, bi, i: (bi, 0, j))],
        out_specs=pl.BlockSpec((1, tm, tn), lambda j, bi, i: (bi, i, j)),
        out_shape=jax.ShapeDtypeStruct((b, s, d), F32),
        compiler_params=_params("arbitrary", "arbitrary", "arbitrary"),
    )(x, w, h, mod)


def _router_kernel(h_ref, mod_ref, g_ref, wr_ref, br_ref, u_ref, idx_ref, wt_ref, *, row):
    y = _rms(h_ref[0]) * g_ref[...]
    u = y * (1.0 + mod_ref[0, row + 1:row + 2, :]) + mod_ref[0, row:row + 1, :]
    u_ref[0] = u
    logits = jnp.dot(u, wr_ref[...], preferred_element_type=F32, precision=lax.Precision.HIGHEST) + br_ref[...]
    ts, ne = logits.shape
    lane = lax.broadcasted_iota(jnp.int32, (ts, ne), 1)
    out_lane = lax.broadcasted_iota(jnp.int32, (ts, TOP_K), 1)
    idx = jnp.zeros((ts, TOP_K), jnp.int32)
    val = jnp.zeros((ts, TOP_K), F32)
    for kk in range(TOP_K):
        m = jnp.max(logits, axis=-1, keepdims=True)
        pick = jnp.min(jnp.where(logits == m, lane, ne), axis=-1, keepdims=True)
        idx = jnp.where(out_lane == kk, pick, idx)
        val = jnp.where(out_lane == kk, m, val)
        logits = jnp.where(lane == pick, -jnp.inf, logits)
    e = jnp.exp(val - jnp.max(val, axis=-1, keepdims=True))
    idx_ref[0] = idx
    wt_ref[0] = e / jnp.sum(e, axis=-1, keepdims=True)


def _router(h, mod, g, w_router, b_router, row):
    b, s, d = h.shape
    ne = w_router.shape[1]
    ts = _tile(s, 512)
    return pl.pallas_call(
        functools.partial(_router_kernel, row=row),
        grid=(b, s // ts),
        in_specs=[pl.BlockSpec((1, ts, d), lambda i, j: (i, j, 0)),
                  pl.BlockSpec((1, 6, d), lambda i, j: (i, 0, 0)),
                  pl.BlockSpec((1, d), lambda i, j: (0, 0)),
                  pl.BlockSpec((d, ne), lambda i, j: (0, 0)),
                  pl.BlockSpec((1, ne), lambda i, j: (0, 0))],
        out_specs=[pl.BlockSpec((1, ts, d), lambda i, j: (i, j, 0)),
                   pl.BlockSpec((1, ts, TOP_K), lambda i, j: (i, j, 0)),
                   pl.BlockSpec((1, ts, TOP_K), lambda i, j: (i, j, 0))],
        out_shape=[jax.ShapeDtypeStruct((b, s, d), F32),
                   jax.ShapeDtypeStruct((b, s, TOP_K), jnp.int32),
                   jax.ShapeDtypeStruct((b, s, TOP_K), F32)],
        compiler_params=_params("arbitrary", "arbitrary"),
    )(h, mod, g.reshape(1, d), w_router, b_router.reshape(1, ne))


def _row_copy(src_hbm, dst, sem, src_row, dst_row):
    return pltpu.make_async_copy(src_hbm.at[src_row], dst.at[dst_row], sem)


def _gather_kernel(tok_ref, x_hbm, o_ref, buf, sem, *, rows):
    def start(r, c):
        _row_copy(x_hbm, buf, sem, tok_ref[0, 0, r], r).start()
        return c

    def wait(r, c):
        _row_copy(x_hbm, buf, sem, 0, r).wait()
        return c

    lax.fori_loop(0, rows, start, 0)
    lax.fori_loop(0, rows, wait, 0)
    o_ref[...] = buf[...].astype(o_ref.dtype)


def _gather_rows(x, row_tok, rows):
    t, d = x.shape
    n_rows = row_tok.shape[0]
    nb = n_rows // rows
    return pl.pallas_call(
        functools.partial(_gather_kernel, rows=rows),
        grid=(nb,),
        in_specs=[pl.BlockSpec((1, 1, rows), lambda i: (i, 0, 0), memory_space=pltpu.SMEM),
                  pl.BlockSpec(memory_space=pl.ANY)],
        out_specs=pl.BlockSpec((rows, d), lambda i: (i, 0)),
        out_shape=jax.ShapeDtypeStruct((n_rows, d), BF16),
        scratch_shapes=[pltpu.VMEM((rows, d), F32), pltpu.SemaphoreType.DMA(())],
        compiler_params=_params("arbitrary"),
    )(row_tok.reshape(nb, 1, rows), x)


def _expert_changed(be_ref):
    i = pl.program_id(1)
    prev = be_ref[jnp.maximum(i - 1, 0)]
    return jnp.logical_or(i == 0, be_ref[i] != prev)


def _expert_up_kernel(be_ref, nu_ref, x_ref, wg_ref, wl_ref, bg_ref, bl_ref, o_ref, wg_bf, wl_bf):
    i = pl.program_id(1)

    @pl.when(_expert_changed(be_ref))
    def _():
        wg_bf[...] = wg_ref[0].astype(BF16)
        wl_bf[...] = wl_ref[0].astype(BF16)

    @pl.when(i < nu_ref[0])
    def _():
        x = x_ref[...]
        glu = jnp.minimum(_dot(x, wg_bf[...]) + bg_ref[0], SWIGLU_LIMIT)
        lin = jnp.clip(_dot(x, wl_bf[...]) + bl_ref[0], -SWIGLU_LIMIT, SWIGLU_LIMIT)
        o_ref[...] = (glu * _sigmoid(SWIGLU_ALPHA * glu) * (lin + 1.0)).astype(o_ref.dtype)

    @pl.when(i >= nu_ref[0])
    def _():
        o_ref[...] = jnp.zeros(o_ref.shape, o_ref.dtype)


def _expert_up(xs, w1, b1, block_e, n_used):
    n_rows, d = xs.shape
    ne, _, ff2 = w1.shape
    ff = ff2 // 2
    tn = _tile(ff, 1024)
    nj = ff // tn
    nb = n_rows // MOE_ROWS
    grid_spec = pltpu.PrefetchScalarGridSpec(
        num_scalar_prefetch=2,
        grid=(nj, nb),
        in_specs=[pl.BlockSpec((MOE_ROWS, d), lambda j, i, be, nu: (i, 0)),
                  pl.BlockSpec((1, d, tn), lambda j, i, be, nu: (be[i], 0, j)),
                  pl.BlockSpec((1, d, tn), lambda j, i, be, nu: (be[i], 0, nj + j)),
                  pl.BlockSpec((1, 1, tn), lambda j, i, be, nu: (be[i], 0, j)),
                  pl.BlockSpec((1, 1, tn), lambda j, i, be, nu: (be[i], 0, nj + j))],
        out_specs=pl.BlockSpec((MOE_ROWS, tn), lambda j, i, be, nu: (i, j)),
        scratch_shapes=[pltpu.VMEM((d, tn), BF16), pltpu.VMEM((d, tn), BF16)])
    b1r = b1.reshape(ne, 1, ff2)
    return pl.pallas_call(
        _expert_up_kernel,
        grid_spec=grid_spec,
        out_shape=jax.ShapeDtypeStruct((n_rows, ff), BF16),
        compiler_params=_params("arbitrary", "arbitrary"),
    )(block_e, n_used, xs, w1, w1, b1r, b1r)


def _expert_down_kernel(be_ref, nu_ref, a_ref, w_ref, b_ref, rw_ref, o_ref, w_bf):
    i = pl.program_id(1)

    @pl.when(_expert_changed(be_ref))
    def _():
        w_bf[...] = w_ref[0].astype(BF16)

    @pl.when(i < nu_ref[0])
    def _():
        o_ref[...] = (_dot(a_ref[...], w_bf[...]) + b_ref[0]) * rw_ref[...]

    @pl.when(i >= nu_ref[0])
    def _():
        o_ref[...] = jnp.zeros(o_ref.shape, o_ref.dtype)


def _expert_down(act, w2, b2, row_w, block_e, n_used):
    n_rows, ff = act.shape
    ne, _, d = w2.shape
    tn = _tile(d, 1024)
    nb = n_rows // MOE_ROWS
    grid_spec = pltpu.PrefetchScalarGridSpec(
        num_scalar_prefetch=2,
        grid=(d // tn, nb),
        in_specs=[pl.BlockSpec((MOE_ROWS, ff), lambda j, i, be, nu: (i, 0)),
                  pl.BlockSpec((1, ff, tn), lambda j, i, be, nu: (be[i], 0, j)),
                  pl.BlockSpec((1, 1, tn), lambda j, i, be, nu: (be[i], 0, j)),
                  pl.BlockSpec((MOE_ROWS, 1), lambda j, i, be, nu: (i, 0))],
        out_specs=pl.BlockSpec((MOE_ROWS, tn), lambda j, i, be, nu: (i, j)),
        scratch_shapes=[pltpu.VMEM((ff, tn), BF16)])
    return pl.pallas_call(
        _expert_down_kernel,
        grid_spec=grid_spec,
        out_shape=jax.ShapeDtypeStruct((n_rows, d), F32),
        compiler_params=_params("arbitrary", "arbitrary"),
    )(block_e, n_used, act, w2, b2.reshape(ne, 1, d), row_w.reshape(n_rows, 1))


def _combine_kernel(pos_ref, y_hbm, h_ref, mod_ref, o_ref, buf, sem, *, tc, row):
    n = TOP_K * tc

    def start(r, c):
        _row_copy(y_hbm, buf, sem, pos_ref[0, 0, r], r).start()
        return c

    def wait(r, c):
        _row_copy(y_hbm, buf, sem, 0, r).wait()
        return c

    lax.fori_loop(0, n, start, 0)
    lax.fori_loop(0, n, wait, 0)
    tot = buf[0:tc, :]
    for kk in range(1, TOP_K):
        tot = tot + buf[kk * tc:(kk + 1) * tc, :]
    o_ref[0] = h_ref[0] + mod_ref[0, row:row + 1, :] * tot


def _combine(ys, pos, h, mod, row):
    b, s, d = h.shape
    tc = _tile(s, 128)
    nt = (b * s) // tc
    spb = s // tc
    pos_blk = pos.reshape(nt, tc, TOP_K).transpose(0, 2, 1).reshape(nt, 1, TOP_K * tc)
    return pl.pallas_call(
        functools.partial(_combine_kernel, tc=tc, row=row),
        grid=(nt,),
        in_specs=[pl.BlockSpec((1, 1, TOP_K * tc), lambda i: (i, 0, 0), memory_space=pltpu.SMEM),
                  pl.BlockSpec(memory_space=pl.ANY),
                  pl.BlockSpec((1, tc, d), lambda i: (i // spb, i % spb, 0)),
                  pl.BlockSpec((1, 6, d), lambda i: (i // spb, 0, 0))],
        out_specs=pl.BlockSpec((1, tc, d), lambda i: (i // spb, i % spb, 0)),
        out_shape=jax.ShapeDtypeStruct((b, s, d), F32),
        scratch_shapes=[pltpu.VMEM((TOP_K * tc, d), F32), pltpu.SemaphoreType.DMA(())],
        compiler_params=_params("arbitrary"),
    )(pos_blk, ys, h, mod)


def _routing_tables(top_idx, top_w, n_experts):
    n_assign = top_idx.size
    e_flat = top_idx.reshape(-1)
    onehot = (e_flat[:, None] == jnp.arange(n_experts, dtype=jnp.int32)[None, :]).astype(jnp.int32)
    csum = jnp.cumsum(onehot, axis=0)
    rank = jnp.take_along_axis(csum, e_flat[:, None], axis=1)[:, 0] - 1
    counts = csum[-1]
    padded = (counts + MOE_ROWS - 1) // MOE_ROWS * MOE_ROWS
    pend = jnp.cumsum(padded)
    dest = (pend - padded)[e_flat] + rank
    n_rows = n_assign + n_experts * MOE_ROWS
    nb = n_rows // MOE_ROWS
    block_e = jnp.minimum(jnp.searchsorted(pend, jnp.arange(nb, dtype=jnp.int32) * MOE_ROWS, side='right'),
                          n_experts - 1).astype(jnp.int32)
    n_used = (pend[-1:] // MOE_ROWS).astype(jnp.int32)
    row_tok = jnp.zeros((n_rows,), jnp.int32).at[dest].set(jnp.arange(n_assign, dtype=jnp.int32) // TOP_K)
    row_w = jnp.zeros((n_rows,), F32).at[dest].set(top_w.reshape(-1))
    return dest.astype(jnp.int32), block_e, n_used, row_tok, row_w


def _moe(h, mod, g, w_router, b_router, w1, b1, w2, b2):
    b, s, d = h.shape
    ne = w_router.shape[1]
    u, top_idx, top_w = _router(h, mod, g, w_router, b_router, row=3)
    dest, block_e, n_used, row_tok, row_w = _routing_tables(top_idx, top_w, ne)
    xs = _gather_rows(u.reshape(b * s, d), row_tok, MOE_ROWS)
    act = _expert_up(xs, w1, b1, block_e, n_used)
    ys = _expert_down(act, w2, b2, row_w, block_e, n_used)
    return _combine(ys, dest.reshape(b * s, TOP_K), h, mod, row=5)


def _rot_cols(w):
    half = w.shape[-1] // 2
    return jnp.concatenate([-w[..., half:], w[..., :half]], axis=-1)


def _pad_cols(w, width):
    return jnp.pad(w, ((0, 0), (0, width - w.shape[1])))


def _mixer_weights(w_in, w_uq, w_ukv, q_lora, kv_lora, n_a, n_b, n_c):
    d = w_in.shape[0]
    o1 = q_lora + kv_lora
    o2 = o1 + QK_ROPE
    o3 = o2 + 3 * n_b * HEAD
    o4 = o3 + 3 * n_c * HEAD
    w_kr = w_in[:, o1:o2]
    w_lat = jnp.concatenate([w_in[:, :o1], _pad_cols(w_kr, LANES), _pad_cols(_rot_cols(w_kr), LANES),
                             _pad_cols(w_in[:, o4:], LANES)], axis=1).astype(BF16)
    w_qkv = w_in[:, o2:o4].astype(BF16)
    qscale = jnp.concatenate([jnp.full((n_b * HEAD,), float(HEAD) ** -0.5, F32), jnp.ones((2 * n_b * HEAD,), F32),
                              jnp.full((n_c * HEAD,), float(HEAD) ** -0.5, F32), jnp.ones((2 * n_c * HEAD,), F32)])
    uq = w_uq.reshape(q_lora, n_a, QK_NOPE + QK_ROPE)
    zeros = jnp.zeros((q_lora, n_a, MLA_QK - QK_NOPE - QK_ROPE), F32)
    wq = jnp.concatenate([uq, zeros], axis=-1).reshape(q_lora, n_a * MLA_QK).astype(BF16)
    wqr = jnp.concatenate([_rot_cols(uq[..., QK_NOPE:]), jnp.zeros((q_lora, n_a, LANES - QK_ROPE), F32)],
                          axis=-1).reshape(q_lora, n_a * LANES).astype(BF16)
    ukv = w_ukv.reshape(kv_lora, n_a, QK_NOPE + HEAD)
    wkv = jnp.concatenate([ukv[..., :QK_NOPE].reshape(kv_lora, -1), ukv[..., QK_NOPE:].reshape(kv_lora, -1)],
                          axis=1).astype(BF16)
    return w_lat, w_qkv, qscale, wq, wqr, wkv


def _rope_tables(positions):
    inv = ROPE_THETA ** (-jnp.arange(0, QK_ROPE, 2, dtype=F32) / QK_ROPE)
    ang = positions.astype(F32).reshape(-1)[:, None] * inv
    pad = jnp.zeros((ang.shape[0], LANES - QK_ROPE), F32)
    cos, sin = jnp.cos(ang), jnp.sin(ang)
    return jnp.concatenate([cos, cos, pad], axis=1), jnp.concatenate([sin, sin, pad], axis=1)


def _mixer(h, mod, ctab, stab, g_norm1, w_in, g_q_lat, w_uq, g_kv_lat, w_ukv, rel_bias, b_forget,
           w_gate, b_gate, w_branch, w_out):
    b, s, d = h.shape
    t = b * s
    q_lora, kv_lora = g_q_lat.shape[0], g_kv_lat.shape[0]
    n_b, n_c = rel_bias.shape[0], b_forget.shape[0]
    n_a = w_ukv.shape[1] // (QK_NOPE + HEAD)
    w_lat, w_qkv, qscale, wq, wqr, wkv = _mixer_weights(w_in, w_uq, w_ukv, q_lora, kv_lora, n_a, n_b, n_c)

    u = _norm_mod(h, mod, g_norm1, row=0).reshape(t, d)
    n_lat = w_lat.shape[1]
    lat = _mm(u, w_lat, jnp.zeros((n_lat,), F32), jnp.ones((n_lat,), F32), F32)
    qkv = _mm(u, w_qkv, jnp.zeros((qscale.shape[0],), F32), qscale, BF16)
    gates = _mm(u, w_gate.astype(BF16), b_gate, jnp.ones_like(b_gate), BF16, act="sigmoid")

    q_cat, k_cat, v_a = _mla_prep(lat, ctab, stab, wq, wqr, wkv, g_q_lat, g_kv_lat, n_a)
    o_a = _causal_attn(q_cat.reshape(b, s, -1), k_cat.reshape(b, s, -1), v_a.reshape(b, s, -1),
                       0, 0, 0, n_a, MLA_QK, CHUNK)
    qkv3 = qkv.reshape(b, s, -1)
    o_b = _band_attn(qkv3, rel_bias, n_b)
    f_off = q_lora + kv_lora + 2 * LANES
    f_t = lat[:, f_off:f_off + n_c].reshape(b, s, n_c).transpose(0, 2, 1).reshape(b * n_c, s)
    fcum = _fgate(f_t, jnp.tile(b_forget.astype(F32), b).reshape(b * n_c, 1))
    c0 = 3 * n_b
    o_c = _causal_attn(qkv3, qkv3, qkv3, c0, c0 + n_c, c0 + 2 * n_c, n_c, HEAD, 1, fcum=fcum)

    merged = _merge(o_a.reshape(t, -1), o_b.reshape(t, -1), o_c.reshape(t, -1), gates, w_branch.astype(BF16))
    return _out_proj(merged.reshape(b, s, d), w_out.astype(BF16), h, mod, row=2)


def kernel(x, c, positions, w_mod, b_mod, g_norm1, g_norm2, w_in, g_q_lat, w_uq, g_kv_lat, w_ukv, rel_bias, b_forget, w_gate, b_gate, w_branch, w_out, w_router, b_router, w_mlp1, b_mlp1, w_mlp2, b_mlp2, g_final):
    depth = w_mod.shape[0]
    ctab, stab = _rope_tables(positions)
    mods = _modulation(c, w_mod, b_mod)
    h = x
    for l in range(depth):
        h = _mixer(h, mods[l], ctab, stab, g_norm1[l], w_in[l], g_q_lat[l], w_uq[l], g_kv_lat[l], w_ukv[l],
                   rel_bias[l], b_forget[l], w_gate[l], b_gate[l], w_branch[l], w_out[l])
        h = _moe(h, mods[l], g_norm2[l], w_router[l], b_router[l], w_mlp1[l], b_mlp1[l], w_mlp2[l], b_mlp2[l])
    return _final_norm(h, g_final)
```

```python
import functools

import jax
import jax.numpy as jnp
from jax import lax
from jax.experimental import pallas as pl
from jax.experimental.pallas import tpu as pltpu

CHUNK = 64
QK_NOPE = 128
QK_ROPE = 64
HEAD = 128
LEFT_CHUNKS = 8
MAX_LEFT = 128
ROPE_THETA = 10000.0
NEG_INF = -1e30
NORM_EPS = 1e-6
TOP_K = 4
TOP_K_LOG2 = 2
SWIGLU_ALPHA = 1.702
SWIGLU_LIMIT = 7.0

LANES = 128
MLA_QK = 2 * LANES
VMEM_LIMIT = 56 * 1024 * 1024
MOE_ROWS = 256

ATTN_TQ = 512
ATTN_TK = 256
ONES_ROWS = 16
LOG2E = 1.4426950408889634

F32 = jnp.float32
BF16 = jnp.bfloat16


def _tile(n, pref):
    t = min(n, pref)
    while n % t:
        t //= 2
    return t


def _params(*sem):
    return pltpu.CompilerParams(dimension_semantics=sem, vmem_limit_bytes=VMEM_LIMIT)


def _dot(a, b):
    return jnp.dot(a, b, preferred_element_type=F32)


def _dot_nt(a, b):
    return lax.dot_general(a, b, (((1,), (1,)), ((), ())), preferred_element_type=F32)


def _sigmoid(x):
    return 1.0 / (1.0 + jnp.exp(-x))


def _mod_kernel(c_ref, w_ref, b_ref, o_ref):
    c = c_ref[...]
    ca = (c * _sigmoid(c)).astype(BF16)
    o_ref[0] = _dot(ca, w_ref[0].astype(BF16)) + b_ref[0]


def _modulation(c, w_mod, b_mod):
    nl, d, n = w_mod.shape
    b = c.shape[0]
    bp = 16
    cp = jnp.zeros((bp, d), F32).at[:b].set(c)
    tn = _tile(n, 1024)
    out = pl.pallas_call(
        _mod_kernel,
        grid=(nl, n // tn),
        in_specs=[pl.BlockSpec((bp, d), lambda l, j: (0, 0)),
                  pl.BlockSpec((1, d, tn), lambda l, j: (l, 0, j)),
                  pl.BlockSpec((1, 1, tn), lambda l, j: (l, 0, j))],
        out_specs=pl.BlockSpec((1, bp, tn), lambda l, j: (l, 0, j)),
        out_shape=jax.ShapeDtypeStruct((nl, bp, n), F32),
        compiler_params=_params("arbitrary", "arbitrary"),
    )(cp, w_mod, b_mod.reshape(nl, 1, n))
    return out[:, :b].reshape(nl, b, 6, d)


def _rms(x):
    return x * lax.rsqrt(jnp.mean(x * x, axis=-1, keepdims=True) + NORM_EPS)


def _norm_mod_kernel(h_ref, mod_ref, g_ref, u_ref, *, row):
    y = _rms(h_ref[0]) * g_ref[...]
    u = y * (1.0 + mod_ref[0, row + 1:row + 2, :]) + mod_ref[0, row:row + 1, :]
    u_ref[0] = u.astype(u_ref.dtype)


def _norm_mod(h, mod, g, row):
    b, s, d = h.shape
    ts = _tile(s, 512)
    return pl.pallas_call(
        functools.partial(_norm_mod_kernel, row=row),
        grid=(b, s // ts),
        in_specs=[pl.BlockSpec((1, ts, d), lambda i, j: (i, j, 0)),
                  pl.BlockSpec((1, 6, d), lambda i, j: (i, 0, 0)),
                  pl.BlockSpec((1, d), lambda i, j: (0, 0))],
        out_specs=pl.BlockSpec((1, ts, d), lambda i, j: (i, j, 0)),
        out_shape=jax.ShapeDtypeStruct((b, s, d), BF16),
        compiler_params=_params("arbitrary", "arbitrary"),
    )(h, mod, g.reshape(1, d))


def _final_norm_kernel(h_ref, g_ref, o_ref):
    o_ref[0] = _rms(h_ref[0]) * g_ref[...]


def _final_norm(h, g):
    b, s, d = h.shape
    ts = _tile(s, 512)
    return pl.pallas_call(
        _final_norm_kernel,
        grid=(b, s // ts),
        in_specs=[pl.BlockSpec((1, ts, d), lambda i, j: (i, j, 0)),
                  pl.BlockSpec((1, d), lambda i, j: (0, 0))],
        out_specs=pl.BlockSpec((1, ts, d), lambda i, j: (i, j, 0)),
        out_shape=jax.ShapeDtypeStruct((b, s, d), F32),
        compiler_params=_params("arbitrary", "arbitrary"),
    )(h, g.reshape(1, d))


def _mm_kernel(x_ref, w_ref, b_ref, s_ref, o_ref, *, act):
    acc = (_dot(x_ref[...], w_ref[...]) + b_ref[...]) * s_ref[...]
    if act == "sigmoid":
        acc = _sigmoid(acc)
    o_ref[...] = acc.astype(o_ref.dtype)


def _mm(x, w, bias, scale, out_dtype, act=None):
    m, k = x.shape
    n = w.shape[1]
    tm = _tile(m, 1024)
    tn = n if n <= 1536 else _tile(n, 1024)
    return pl.pallas_call(
        functools.partial(_mm_kernel, act=act),
        grid=(n // tn, m // tm),
        in_specs=[pl.BlockSpec((tm, k), lambda j, i: (i, 0)),
                  pl.BlockSpec((k, tn), lambda j, i: (0, j)),
                  pl.BlockSpec((1, tn), lambda j, i: (0, j)),
                  pl.BlockSpec((1, tn), lambda j, i: (0, j))],
        out_specs=pl.BlockSpec((tm, tn), lambda j, i: (i, j)),
        out_shape=jax.ShapeDtypeStruct((m, n), out_dtype),
        compiler_params=_params("arbitrary", "arbitrary"),
    )(x, w, bias.reshape(1, n), scale.reshape(1, n))


def _mla_prep_kernel(lat_ref, c_ref, s_ref, wq_ref, wqr_ref, wkv_ref, gq_ref, gkv_ref,
                     q_ref, k_ref, v_ref, *, q_lora, kv_lora, n_heads, scale):
    cos = c_ref[...]
    sin = s_ref[...]
    qn = (_rms(lat_ref[:, :q_lora]) * gq_ref[...]).astype(BF16)
    qa = _dot(qn, wq_ref[...])
    qr = _dot(qn, wqr_ref[...])
    kvn = (_rms(lat_ref[:, q_lora:q_lora + kv_lora]) * gkv_ref[...]).astype(BF16)
    kv = _dot(kvn, wkv_ref[...])
    o = q_lora + kv_lora
    k_rope = (lat_ref[:, o:o + LANES] * cos + lat_ref[:, o + LANES:o + 2 * LANES] * sin).astype(BF16)
    for h in range(n_heads):
        a = h * MLA_QK
        q_ref[:, a:a + LANES] = (qa[:, a:a + LANES] * scale).astype(BF16)
        rope = qa[:, a + LANES:a + MLA_QK] * cos + qr[:, h * LANES:(h + 1) * LANES] * sin
        q_ref[:, a + LANES:a + MLA_QK] = (rope * scale).astype(BF16)
        k_ref[:, a:a + LANES] = kv[:, h * LANES:(h + 1) * LANES].astype(BF16)
        k_ref[:, a + LANES:a + MLA_QK] = k_rope
    v_ref[...] = kv[:, n_heads * LANES:].astype(BF16)


def _mla_prep(lat, ctab, stab, wq, wqr, wkv, gq, gkv, n_heads):
    t, nl = lat.shape
    q_lora, kv_lora = gq.shape[0], gkv.shape[0]
    tm = _tile(t, 512)
    kern = functools.partial(_mla_prep_kernel, q_lora=q_lora, kv_lora=kv_lora, n_heads=n_heads,
                             scale=LOG2E * float(QK_NOPE + QK_ROPE) ** -0.5)
    full = lambda a: pl.BlockSpec(a.shape, lambda i: (0,) * a.ndim)
    gq2, gkv2 = gq.reshape(1, -1), gkv.reshape(1, -1)
    return pl.pallas_call(
        kern,
        grid=(t // tm,),
        in_specs=[pl.BlockSpec((tm, nl), lambda i: (i, 0)),
                  pl.BlockSpec((tm, LANES), lambda i: (i, 0)),
                  pl.BlockSpec((tm, LANES), lambda i: (i, 0)),
                  full(wq), full(wqr), full(wkv), full(gq2), full(gkv2)],
        out_specs=[pl.BlockSpec((tm, n_heads * MLA_QK), lambda i: (i, 0)),
                   pl.BlockSpec((tm, n_heads * MLA_QK), lambda i: (i, 0)),
                   pl.BlockSpec((tm, n_heads * HEAD), lambda i: (i, 0))],
        out_shape=[jax.ShapeDtypeStruct((t, n_heads * MLA_QK), BF16),
                   jax.ShapeDtypeStruct((t, n_heads * MLA_QK), BF16),
                   jax.ShapeDtypeStruct((t, n_heads * HEAD), BF16)],
        compiler_params=_params("arbitrary"),
    )(lat, ctab, stab, wq, wqr, wkv, gq2, gkv2)


def _fgate_kernel(f_ref, b_ref, o_ref):
    x = f_ref[...] + b_ref[...]
    v = jnp.minimum(x, 0.0) - jnp.log(1.0 + jnp.exp(-jnp.abs(x)))
    n = v.shape[1]
    lane = lax.broadcasted_iota(jnp.int32, v.shape, 1)
    sh = 1
    while sh < n:
        v = v + jnp.where(lane >= sh, pltpu.roll(v, sh, 1), 0.0)
        sh *= 2
    o_ref[...] = v * LOG2E


def _fgate(f_t, bias):
    r, s = f_t.shape
    return pl.pallas_call(
        _fgate_kernel,
        grid=(1,),
        in_specs=[pl.BlockSpec((r, s), lambda i: (0, 0)), pl.BlockSpec((r, 1), lambda i: (0, 0))],
        out_specs=pl.BlockSpec((r, s), lambda i: (0, 0)),
        out_shape=jax.ShapeDtypeStruct((r, s), F32),
        compiler_params=_params("arbitrary"),
    )(f_t, bias)


def _attn_kernel(*refs, tq, tk, mode):
    if mode == "chunk":
        q_ref, k_ref, v_ref, o_ref, vt_ref, s0, s1, c0, c1, m_ref, acc_ref = refs
        x_ref = None
    else:
        q_ref, k_ref, v_ref, x_ref, o_ref, vt_ref, s0, s1, c0, c1, m_ref, acc_ref = refs
    i = pl.program_id(2)
    sbuf, cbuf = (s0, s1), (c0, c1)
    r = tq // tk

    @pl.when(i == 0)
    def _():
        vt_ref[0:HEAD, :] = v_ref[0].astype(F32).T.astype(BF16)
        vt_ref[HEAD:, :] = jnp.ones((ONES_ROWS, vt_ref.shape[1]), BF16)

    q = q_ref[0]
    m_ref[...] = jnp.full(m_ref.shape, NEG_INF, F32)
    acc_ref[...] = jnp.zeros(acc_ref.shape, F32)

    def stage_a(j, slot, diag_off=None, bias_idx=None, penalty=None):
        start = pl.multiple_of(j * tk, tk)
        s = _dot_nt(k_ref[0, pl.ds(start, tk), :], q)
        if mode == "decay":
            s = s - jnp.tile(x_ref[0, pl.ds(start, tk), :], (1, tq // LANES))
        if mode == "band":
            s = s + x_ref[0, bias_idx]
            if penalty is not None:
                s = s + penalty
        elif diag_off is not None:
            row = lax.broadcasted_iota(jnp.int32, (tk, tq), 0) + diag_off
            col = lax.broadcasted_iota(jnp.int32, (tk, tq), 1)
            if mode == "chunk":
                row, col = row // CHUNK, col // CHUNK
            s = jnp.where(row <= col, s, NEG_INF)
        sbuf[slot][...] = s
        cbuf[slot][...] = jnp.max(s, axis=0, keepdims=True)

    def stage_b(j, slot):
        start = pl.multiple_of(j * tk, tk)
        m_prev = m_ref[...]
        m_new = jnp.maximum(m_prev, cbuf[slot][...])
        alpha = jnp.exp2(m_prev - m_new)
        p = jnp.exp2(sbuf[slot][...] - m_new).astype(BF16)
        acc_ref[...] = alpha * acc_ref[...] + _dot(vt_ref[:, pl.ds(start, tk)], p)
        m_ref[...] = m_new

    if mode == "band":
        penalty = jnp.where(i == 0, NEG_INF, 0.0).astype(F32)
        blk = [jnp.maximum(r * i - 2 + d, 0) for d in range(4)]
        stage_a(blk[0], 0, bias_idx=0, penalty=penalty)
        stage_a(blk[1], 1, bias_idx=1, penalty=penalty)
        stage_b(blk[0], 0)
        stage_a(blk[2], 0, bias_idx=2)
        stage_b(blk[1], 1)
        stage_a(blk[3], 1, bias_idx=3)
        stage_b(blk[2], 0)
        stage_b(blk[3], 1)
    else:
        stage_a(r * i, 0, diag_off=0)
        stage_a(r * i + 1, 1, diag_off=tk)
        stage_b(r * i, 0)

        def body(p, carry):
            stage_a(2 * p, 0)
            stage_b(jnp.where(p == 0, r * i + 1, 2 * p - 1), 1)
            stage_a(2 * p + 1, 1)
            stage_b(2 * p, 0)
            return carry

        lax.fori_loop(0, i, body, 0)
        stage_b(jnp.where(i == 0, 1, r * i - 1), 1)
    acc = acc_ref[...]
    o_ref[0] = (acc[:HEAD] / acc[HEAD:HEAD + 1]).T.astype(o_ref.dtype)


def _band_bias(rel_bias, tq, tk):
    qpos = jnp.arange(tq)[None, :]
    tabs = []
    for d in range(4):
        kpos = jnp.arange(tk)[:, None] + (d - 2) * tk
        rel = kpos - qpos
        dchunk = kpos // CHUNK - qpos // CHUNK
        bias = rel_bias[:, jnp.clip(rel, -MAX_LEFT, CHUNK - 1) + MAX_LEFT].astype(F32) * LOG2E
        tabs.append(jnp.where((dchunk <= 0) & (dchunk >= -LEFT_CHUNKS), bias, NEG_INF))
    return jnp.stack(tabs, axis=1)


def _attn(q_arr, k_arr, v_arr, q_off, k_off, v_off, n_heads, dk, mode, extra=None):
    b, s, _ = q_arr.shape
    tq, tk = ATTN_TQ, ATTN_TK
    assert s % tq == 0 and tq == 2 * tk and LEFT_CHUNKS * CHUNK == 2 * tk
    in_specs = [pl.BlockSpec((1, tq, dk), lambda bi, h, i: (bi, i, q_off + h)),
                pl.BlockSpec((1, s, dk), lambda bi, h, i: (bi, 0, k_off + h)),
                pl.BlockSpec((1, s, HEAD), lambda bi, h, i: (bi, 0, v_off + h))]
    args = [q_arr, k_arr, v_arr]
    if mode == "decay":
        in_specs.append(pl.BlockSpec((1, s, LANES), lambda bi, h, i: (bi * n_heads + h, 0, 0)))
        args.append(extra)
    if mode == "band":
        in_specs.append(pl.BlockSpec((1, 4, tk, tq), lambda bi, h, i: (h, 0, 0, 0)))
        args.append(extra)
    return pl.pallas_call(
        functools.partial(_attn_kernel, tq=tq, tk=tk, mode=mode),
        grid=(b, n_heads, s // tq),
        in_specs=in_specs,
        out_specs=pl.BlockSpec((1, tq, HEAD), lambda bi, h, i: (bi, i, h)),
        out_shape=jax.ShapeDtypeStruct((b, s, n_heads * HEAD), BF16),
        scratch_shapes=[pltpu.VMEM((HEAD + ONES_ROWS, s), BF16),
                        pltpu.VMEM((tk, tq), F32), pltpu.VMEM((tk, tq), F32),
                        pltpu.VMEM((1, tq), F32), pltpu.VMEM((1, tq), F32),
                        pltpu.VMEM((1, tq), F32), pltpu.VMEM((HEAD + ONES_ROWS, tq), F32)],
        compiler_params=_params("arbitrary", "arbitrary", "arbitrary"),
    )(*args)


def _merge_kernel(oa_ref, ob_ref, oc_ref, ga_ref, gb_ref, gc_ref, wa_ref, wb_ref, wc_ref, o_ref):
    acc = ga_ref[...].astype(F32) * _dot(oa_ref[...], wa_ref[0])
    acc += gb_ref[...].astype(F32) * _dot(ob_ref[...], wb_ref[0])
    acc += gc_ref[...].astype(F32) * _dot(oc_ref[...], wc_ref[0])
    o_ref[...] = acc.astype(o_ref.dtype)


def _merge(oa, ob, oc, gates, w_branch):
    t, bw = oa.shape
    d = w_branch.shape[2]
    tm, tn = _tile(t, 512), _tile(d, 1024)
    nj = d // tn
    o_spec = pl.BlockSpec((tm, bw), lambda j, i: (i, 0))
    g_spec = lambda g: pl.BlockSpec((tm, tn), lambda j, i: (i, g * nj + j))
    w_spec = lambda g: pl.BlockSpec((1, bw, tn), lambda j, i: (g, 0, j))
    return pl.pallas_call(
        _merge_kernel,
        grid=(nj, t // tm),
        in_specs=[o_spec, o_spec, o_spec, g_spec(0), g_spec(1), g_spec(2), w_spec(0), w_spec(1), w_spec(2)],
        out_specs=pl.BlockSpec((tm, tn), lambda j, i: (i, j)),
        out_shape=jax.ShapeDtypeStruct((t, d), BF16),
        compiler_params=_params("arbitrary", "arbitrary"),
    )(oa, ob, oc, gates, gates, gates, w_branch, w_branch, w_branch)


def _out_proj_kernel(x_ref, w_ref, h_ref, mod_ref, o_ref, *, row):
    o_ref[0] = h_ref[0] + mod_ref[0, row:row + 1, :] * _dot(x_ref[0], w_ref[...])


def _out_proj(x, w, h, mod, row):
    b, s, k = x.shape
    d = w.shape[1]
    tm, tn = _tile(s, 1024), _tile(d, 1024)
    return pl.pallas_call(
        functools.partial(_out_proj_kernel, row=row),
        grid=(d // tn, b, s // tm),
        in_specs=[pl.BlockSpec((1, tm, k), lambda j, bi, i: (bi, i, 0)),
                  pl.BlockSpec((k, tn), lambda j, bi, i: (0, j)),
                  pl.BlockSpec((1, tm, tn), lambda j, bi, i: (bi, i, j)),
                  pl.BlockSpec((1, 6, tn), lambda j, bi, i: (bi, 0, j))],
        out_specs=pl.BlockSpec((1, tm, tn), lambda j, bi, i: (bi, i, j)),
        out_shape=jax.ShapeDtypeStruct((b, s, d), F32),
        compiler_params=_params("arbitrary", "arbitrary", "arbitrary"),
    )(x, w, h, mod)


def _router_kernel(h_ref, mod_ref, g_ref, wr_ref, br_ref, u_ref, idx_ref, wt_ref, rank_ref, cnt_ref, carry, *, row):
    first = jnp.logical_and(pl.program_id(0) == 0, pl.program_id(1) == 0)

    @pl.when(first)
    def _():
        carry[...] = jnp.zeros(carry.shape, F32)

    y = _rms(h_ref[0]) * g_ref[...]
    u = y * (1.0 + mod_ref[0, row + 1:row + 2, :]) + mod_ref[0, row:row + 1, :]
    u_ref[0] = u
    logits = jnp.dot(u, wr_ref[...], preferred_element_type=F32, precision=lax.Precision.HIGHEST) + br_ref[...]
    ts, ne = logits.shape
    lane = lax.broadcasted_iota(jnp.int32, (ts, ne), 1)
    out_lane = lax.broadcasted_iota(jnp.int32, (ts, TOP_K), 1)
    idx = jnp.zeros((ts, TOP_K), jnp.int32)
    val = jnp.zeros((ts, TOP_K), F32)
    onehots = []
    for kk in range(TOP_K):
        m = jnp.max(logits, axis=-1, keepdims=True)
        pick = jnp.min(jnp.where(logits == m, lane, ne), axis=-1, keepdims=True)
        idx = jnp.where(out_lane == kk, pick, idx)
        val = jnp.where(out_lane == kk, m, val)
        onehots.append(lane == pick)
        logits = jnp.where(lane == pick, -jnp.inf, logits)
    e = jnp.exp(val - jnp.max(val, axis=-1, keepdims=True))
    idx_ref[0] = idx
    wt_ref[0] = e / jnp.sum(e, axis=-1, keepdims=True)

    hits = jnp.zeros((ts, ne), F32)
    for oh in onehots:
        hits = hits + oh.astype(F32)
    earlier = lax.broadcasted_iota(jnp.int32, (ts, ts), 1) < lax.broadcasted_iota(jnp.int32, (ts, ts), 0)
    before = _dot(earlier.astype(F32).astype(BF16), hits.astype(BF16)) + carry[...]
    rank = jnp.zeros((ts, TOP_K), F32)
    for kk, oh in enumerate(onehots):
        rank = jnp.where(out_lane == kk, jnp.sum(jnp.where(oh, before, 0.0), axis=-1, keepdims=True), rank)
    rank_ref[0] = rank.astype(jnp.int32)
    carry[...] = carry[...] + jnp.sum(hits, axis=0, keepdims=True)
    cnt_ref[...] = carry[...].astype(jnp.int32)


def _router(h, mod, g, w_router, b_router, row):
    b, s, d = h.shape
    ne = w_router.shape[1]
    ts = _tile(s, 512)
    small = lambda: pl.BlockSpec((1, ts, TOP_K), lambda i, j: (i, j, 0))
    return pl.pallas_call(
        functools.partial(_router_kernel, row=row),
        grid=(b, s // ts),
        in_specs=[pl.BlockSpec((1, ts, d), lambda i, j: (i, j, 0)),
                  pl.BlockSpec((1, 6, d), lambda i, j: (i, 0, 0)),
                  pl.BlockSpec((1, d), lambda i, j: (0, 0)),
                  pl.BlockSpec((d, ne), lambda i, j: (0, 0)),
                  pl.BlockSpec((1, ne), lambda i, j: (0, 0))],
        out_specs=[pl.BlockSpec((1, ts, d), lambda i, j: (i, j, 0)), small(), small(), small(),
                   pl.BlockSpec((1, ne), lambda i, j: (0, 0))],
        out_shape=[jax.ShapeDtypeStruct((b, s, d), F32),
                   jax.ShapeDtypeStruct((b, s, TOP_K), jnp.int32),
                   jax.ShapeDtypeStruct((b, s, TOP_K), F32),
                   jax.ShapeDtypeStruct((b, s, TOP_K), jnp.int32),
                   jax.ShapeDtypeStruct((1, ne), jnp.int32)],
        scratch_shapes=[pltpu.VMEM((1, ne), F32)],
        compiler_params=_params("arbitrary", "arbitrary"),
    )(h, mod, g.reshape(1, d), w_router, b_router.reshape(1, ne))


DMA_UNROLL = 8


def _dispatch_kernel(dest_ref, u_ref, xs_in, xs_out, sem, *, n):
    del xs_in

    def copy(r, dst_row):
        return pltpu.make_async_copy(u_ref.at[lax.shift_right_logical(r, TOP_K_LOG2)], xs_out.at[dst_row], sem)

    def start(r, c):
        copy(r, dest_ref[0, 0, r]).start()
        return c

    def wait(r, c):
        copy(r, 0).wait()
        return c

    lax.fori_loop(0, n, start, 0, unroll=DMA_UNROLL)
    lax.fori_loop(0, n, wait, 0, unroll=DMA_UNROLL)


def _dispatch(u, dest, n_rows):
    t, d = u.shape
    tc = _tile(t, 128)
    nt = t // tc
    n = TOP_K * tc
    return pl.pallas_call(
        functools.partial(_dispatch_kernel, n=n),
        grid=(nt,),
        in_specs=[pl.BlockSpec((1, 1, n), lambda i: (i, 0, 0), memory_space=pltpu.SMEM),
                  pl.BlockSpec((tc, d), lambda i: (i, 0)),
                  pl.BlockSpec(memory_space=pl.ANY)],
        out_specs=pl.BlockSpec(memory_space=pl.ANY),
        out_shape=jax.ShapeDtypeStruct((n_rows, d), F32),
        scratch_shapes=[pltpu.SemaphoreType.DMA(())],
        input_output_aliases={2: 0},
        compiler_params=_params("arbitrary"),
    )(dest.reshape(nt, 1, n), u, jnp.zeros((n_rows, d), F32))


def _expert_changed(be_ref):
    i = pl.program_id(1)
    prev = be_ref[jnp.maximum(i - 1, 0)]
    return jnp.logical_or(i == 0, be_ref[i] != prev)


def _expert_up_kernel(be_ref, nu_ref, x_ref, wg_ref, wl_ref, bg_ref, bl_ref, o_ref, wg_bf, wl_bf):
    i = pl.program_id(1)

    @pl.when(_expert_changed(be_ref))
    def _():
        wg_bf[...] = wg_ref[0, 0].astype(BF16)
        wl_bf[...] = wl_ref[0, 0].astype(BF16)

    @pl.when(i < nu_ref[0])
    def _():
        x = x_ref[...].astype(BF16)
        glu = jnp.minimum(_dot(x, wg_bf[...]) + bg_ref[0, 0], SWIGLU_LIMIT)
        lin = jnp.clip(_dot(x, wl_bf[...]) + bl_ref[0, 0], -SWIGLU_LIMIT, SWIGLU_LIMIT)
        o_ref[...] = (glu * _sigmoid(SWIGLU_ALPHA * glu) * (lin + 1.0)).astype(o_ref.dtype)

    @pl.when(i >= nu_ref[0])
    def _():
        o_ref[...] = jnp.zeros(o_ref.shape, o_ref.dtype)


def _expert_up(xs, w1, b1, layer, block_e, n_used):
    n_rows, d = xs.shape
    nl, ne, _, ff2 = w1.shape
    ff = ff2 // 2
    tn = _tile(ff, 1024)
    nj = ff // tn
    nb = n_rows // MOE_ROWS
    grid_spec = pltpu.PrefetchScalarGridSpec(
        num_scalar_prefetch=2,
        grid=(nj, nb),
        in_specs=[pl.BlockSpec((MOE_ROWS, d), lambda j, i, be, nu: (i, 0)),
                  pl.BlockSpec((1, 1, d, tn), lambda j, i, be, nu: (layer, be[i], 0, j)),
                  pl.BlockSpec((1, 1, d, tn), lambda j, i, be, nu: (layer, be[i], 0, nj + j)),
                  pl.BlockSpec((1, 1, 1, tn), lambda j, i, be, nu: (layer, be[i], 0, j)),
                  pl.BlockSpec((1, 1, 1, tn), lambda j, i, be, nu: (layer, be[i], 0, nj + j))],
        out_specs=pl.BlockSpec((MOE_ROWS, tn), lambda j, i, be, nu: (i, j)),
        scratch_shapes=[pltpu.VMEM((d, tn), BF16), pltpu.VMEM((d, tn), BF16)])
    b1r = b1.reshape(nl, ne, 1, ff2)
    return pl.pallas_call(
        _expert_up_kernel,
        grid_spec=grid_spec,
        out_shape=jax.ShapeDtypeStruct((n_rows, ff), BF16),
        compiler_params=_params("arbitrary", "arbitrary"),
    )(block_e, n_used, xs, w1, w1, b1r, b1r)


def _expert_down_kernel(be_ref, nu_ref, a_ref, w_ref, b_ref, o_ref, w_bf):
    i = pl.program_id(1)

    @pl.when(_expert_changed(be_ref))
    def _():
        w_bf[...] = w_ref[0, 0].astype(BF16)

    @pl.when(i < nu_ref[0])
    def _():
        o_ref[...] = _dot(a_ref[...], w_bf[...]) + b_ref[0, 0]

    @pl.when(i >= nu_ref[0])
    def _():
        o_ref[...] = jnp.zeros(o_ref.shape, o_ref.dtype)


def _expert_down(act, w2, b2, layer, block_e, n_used):
    n_rows, ff = act.shape
    nl, ne, _, d = w2.shape
    tn = _tile(d, 1024)
    nb = n_rows // MOE_ROWS
    grid_spec = pltpu.PrefetchScalarGridSpec(
        num_scalar_prefetch=2,
        grid=(d // tn, nb),
        in_specs=[pl.BlockSpec((MOE_ROWS, ff), lambda j, i, be, nu: (i, 0)),
                  pl.BlockSpec((1, 1, ff, tn), lambda j, i, be, nu: (layer, be[i], 0, j)),
                  pl.BlockSpec((1, 1, 1, tn), lambda j, i, be, nu: (layer, be[i], 0, j))],
        out_specs=pl.BlockSpec((MOE_ROWS, tn), lambda j, i, be, nu: (i, j)),
        scratch_shapes=[pltpu.VMEM((ff, tn), BF16)])
    return pl.pallas_call(
        _expert_down_kernel,
        grid_spec=grid_spec,
        out_shape=jax.ShapeDtypeStruct((n_rows, d), F32),
        compiler_params=_params("arbitrary", "arbitrary"),
    )(block_e, n_used, act, w2, b2.reshape(nl, ne, 1, d))


def _combine_kernel(pos_ref, y_hbm, wt_ref, h_ref, mod_ref, o_ref, buf, sem, *, tc, row):
    n = TOP_K * tc

    def copy(r, src_row):
        return pltpu.make_async_copy(y_hbm.at[src_row], buf.at[r], sem)

    def start(r, c):
        copy(r, pos_ref[0, 0, r]).start()
        return c

    def wait(r, c):
        copy(r, 0).wait()
        return c

    lax.fori_loop(0, n, start, 0, unroll=DMA_UNROLL)
    lax.fori_loop(0, n, wait, 0, unroll=DMA_UNROLL)
    wt = wt_ref[...]
    tot = wt[:, 0:1] * buf[0:tc, :]
    for kk in range(1, TOP_K):
        tot = tot + wt[:, kk:kk + 1] * buf[kk * tc:(kk + 1) * tc, :]
    o_ref[0] = h_ref[0] + mod_ref[0, row:row + 1, :] * tot


def _combine(ys, pos, wts, h, mod, row):
    b, s, d = h.shape
    tc = _tile(s, 128)
    nt = (b * s) // tc
    spb = s // tc
    pos_blk = pos.reshape(nt, tc, TOP_K).transpose(0, 2, 1).reshape(nt, 1, TOP_K * tc)
    return pl.pallas_call(
        functools.partial(_combine_kernel, tc=tc, row=row),
        grid=(nt,),
        in_specs=[pl.BlockSpec((1, 1, TOP_K * tc), lambda i: (i, 0, 0), memory_space=pltpu.SMEM),
                  pl.BlockSpec(memory_space=pl.ANY),
                  pl.BlockSpec((tc, TOP_K), lambda i: (i, 0)),
                  pl.BlockSpec((1, tc, d), lambda i: (i // spb, i % spb, 0)),
                  pl.BlockSpec((1, 6, d), lambda i: (i // spb, 0, 0))],
        out_specs=pl.BlockSpec((1, tc, d), lambda i: (i // spb, i % spb, 0)),
        out_shape=jax.ShapeDtypeStruct((b, s, d), F32),
        scratch_shapes=[pltpu.VMEM((TOP_K * tc, d), F32), pltpu.SemaphoreType.DMA(())],
        compiler_params=_params("arbitrary"),
    )(pos_blk, ys, wts, h, mod)


def _routing_tables(top_idx, rank, counts, n_experts):
    padded = (counts + MOE_ROWS - 1) // MOE_ROWS * MOE_ROWS
    pend = jnp.cumsum(padded)
    pstart = pend - padded
    experts = jnp.arange(n_experts, dtype=jnp.int32)
    dest = rank + jnp.sum(jnp.where(top_idx[..., None] == experts, pstart, 0), axis=-1)
    n_rows = top_idx.size + n_experts * MOE_ROWS
    nb = n_rows // MOE_ROWS
    starts = jnp.arange(nb, dtype=jnp.int32) * MOE_ROWS
    block_e = jnp.minimum(jnp.sum((pend[None, :] <= starts[:, None]).astype(jnp.int32), axis=1), n_experts - 1)
    n_used = (pend[-1:] // MOE_ROWS).astype(jnp.int32)
    return dest.astype(jnp.int32), block_e.astype(jnp.int32), n_used, n_rows


def _moe(h, mod, g, w_router, b_router, w1, b1, w2, b2, layer):
    b, s, d = h.shape
    t = b * s
    ne = w_router.shape[1]
    u, top_idx, top_w, rank, counts = _router(h, mod, g, w_router, b_router, row=3)
    dest, block_e, n_used, n_rows = _routing_tables(top_idx.reshape(t, TOP_K), rank.reshape(t, TOP_K),
                                                    counts.reshape(ne), ne)
    xs = _dispatch(u.reshape(t, d), dest, n_rows)
    act = _expert_up(xs, w1, b1, layer, block_e, n_used)
    ys = _expert_down(act, w2, b2, layer, block_e, n_used)
    return _combine(ys, dest, top_w.reshape(t, TOP_K), h, mod, row=5)


def _rot_cols(w):
    half = w.shape[-1] // 2
    return jnp.concatenate([-w[..., half:], w[..., :half]], axis=-1)


def _pad_cols(w, width):
    return jnp.pad(w, ((0, 0), (0, width - w.shape[1])))


def _mixer_weights(w_in, w_uq, w_ukv, q_lora, kv_lora, n_a, n_b, n_c):
    d = w_in.shape[0]
    o1 = q_lora + kv_lora
    o2 = o1 + QK_ROPE
    o3 = o2 + 3 * n_b * HEAD
    o4 = o3 + 3 * n_c * HEAD
    w_kr = w_in[:, o1:o2]
    w_lat = jnp.concatenate([w_in[:, :o1], _pad_cols(w_kr, LANES), _pad_cols(_rot_cols(w_kr), LANES),
                             _pad_cols(w_in[:, o4:], LANES)], axis=1).astype(BF16)
    w_qkv = w_in[:, o2:o4].astype(BF16)
    qscale = jnp.concatenate([jnp.full((n_b * HEAD,), LOG2E * float(HEAD) ** -0.5, F32), jnp.ones((2 * n_b * HEAD,), F32),
                              jnp.full((n_c * HEAD,), LOG2E * float(HEAD) ** -0.5, F32), jnp.ones((2 * n_c * HEAD,), F32)])
    uq = w_uq.reshape(q_lora, n_a, QK_NOPE + QK_ROPE)
    zeros = jnp.zeros((q_lora, n_a, MLA_QK - QK_NOPE - QK_ROPE), F32)
    wq = jnp.concatenate([uq, zeros], axis=-1).reshape(q_lora, n_a * MLA_QK).astype(BF16)
    wqr = jnp.concatenate([_rot_cols(uq[..., QK_NOPE:]), jnp.zeros((q_lora, n_a, LANES - QK_ROPE), F32)],
                          axis=-1).reshape(q_lora, n_a * LANES).astype(BF16)
    ukv = w_ukv.reshape(kv_lora, n_a, QK_NOPE + HEAD)
    wkv = jnp.concatenate([ukv[..., :QK_NOPE].reshape(kv_lora, -1), ukv[..., QK_NOPE:].reshape(kv_lora, -1)],
                          axis=1).astype(BF16)
    return w_lat, w_qkv, qscale, wq, wqr, wkv


def _rope_tables(positions):
    inv = ROPE_THETA ** (-jnp.arange(0, QK_ROPE, 2, dtype=F32) / QK_ROPE)
    ang = positions.astype(F32).reshape(-1)[:, None] * inv
    pad = jnp.zeros((ang.shape[0], LANES - QK_ROPE), F32)
    cos, sin = jnp.cos(ang), jnp.sin(ang)
    return jnp.concatenate([cos, cos, pad], axis=1), jnp.concatenate([sin, sin, pad], axis=1)


def _mixer(h, mod, ctab, stab, g_norm1, w_in, g_q_lat, w_uq, g_kv_lat, w_ukv, rel_bias, b_forget,
           w_gate, b_gate, w_branch, w_out):
    b, s, d = h.shape
    t = b * s
    q_lora, kv_lora = g_q_lat.shape[0], g_kv_lat.shape[0]
    n_b, n_c = rel_bias.shape[0], b_forget.shape[0]
    n_a = w_ukv.shape[1] // (QK_NOPE + HEAD)
    w_lat, w_qkv, qscale, wq, wqr, wkv = _mixer_weights(w_in, w_uq, w_ukv, q_lora, kv_lora, n_a, n_b, n_c)

    u = _norm_mod(h, mod, g_norm1, row=0).reshape(t, d)
    n_lat = w_lat.shape[1]
    lat = _mm(u, w_lat, jnp.zeros((n_lat,), F32), jnp.ones((n_lat,), F32), F32)
    qkv = _mm(u, w_qkv, jnp.zeros((qscale.shape[0],), F32), qscale, BF16)
    gates = _mm(u, w_gate.astype(BF16), b_gate, jnp.ones_like(b_gate), BF16, act="sigmoid")

    q_cat, k_cat, v_a = _mla_prep(lat, ctab, stab, wq, wqr, wkv, g_q_lat, g_kv_lat, n_a)
    o_a = _attn(q_cat.reshape(b, s, -1), k_cat.reshape(b, s, -1), v_a.reshape(b, s, -1), 0, 0, 0, n_a, MLA_QK, "chunk")
    qkv3 = qkv.reshape(b, s, -1)
    o_b = _attn(qkv3, qkv3, qkv3, 0, n_b, 2 * n_b, n_b, HEAD, "band", _band_bias(rel_bias, ATTN_TQ, ATTN_TK))
    f_off = q_lora + kv_lora + 2 * LANES
    f_t = lat[:, f_off:f_off + n_c].reshape(b, s, n_c).transpose(0, 2, 1).reshape(b * n_c, s)
    fcum = _fgate(f_t, jnp.tile(b_forget.astype(F32), b).reshape(b * n_c, 1))
    c0 = 3 * n_b
    f_rep = jnp.broadcast_to(fcum.reshape(b * n_c, s, 1), (b * n_c, s, LANES))
    o_c = _attn(qkv3, qkv3, qkv3, c0, c0 + n_c, c0 + 2 * n_c, n_c, HEAD, "decay", f_rep)

    merged = _merge(o_a.reshape(t, -1), o_b.reshape(t, -1), o_c.reshape(t, -1), gates, w_branch.astype(BF16))
    return _out_proj(merged.reshape(b, s, d), w_out.astype(BF16), h, mod, row=2)


def kernel(x, c, positions, w_mod, b_mod, g_norm1, g_norm2, w_in, g_q_lat, w_uq, g_kv_lat, w_ukv, rel_bias, b_forget, w_gate, b_gate, w_branch, w_out, w_router, b_router, w_mlp1, b_mlp1, w_mlp2, b_mlp2, g_final):
    depth = w_mod.shape[0]
    ctab, stab = _rope_tables(positions)
    mods = _modulation(c, w_mod, b_mod)
    h = x
    for l in range(depth):
        h = _mixer(h, mods[l], ctab, stab, g_norm1[l], w_in[l], g_q_lat[l], w_uq[l], g_kv_lat[l], w_ukv[l],
                   rel_bias[l], b_forget[l], w_gate[l], b_gate[l], w_branch[l], w_out[l])
        h = _moe(h, mods[l], g_norm2[l], w_router[l], b_router[l], w_mlp1, b_mlp1, w_mlp2, b_mlp2, l)
    return _final_norm(h, g_final)
```

```python
import functools

import jax
import jax.numpy as jnp
from jax import lax
from jax.experimental import pallas as pl
from jax.experimental.pallas import tpu as pltpu

CHUNK = 64
QK_NOPE = 128
QK_ROPE = 64
HEAD = 128
LEFT_CHUNKS = 8
MAX_LEFT = 128
ROPE_THETA = 10000.0
NEG_INF = -1e30
NORM_EPS = 1e-6
TOP_K = 4
TOP_K_LOG2 = 2
SWIGLU_ALPHA = 1.702
SWIGLU_LIMIT = 7.0

LANES = 128
MLA_QK = 2 * LANES
VMEM_LIMIT = 56 * 1024 * 1024
MOE_ROWS = 256

ATTN_TILES = {"chunk": (1024, 512), "decay": (1024, 512), "band": (512, 256)}
BAND_LAG = 2
ONES_ROWS = 16
LOG2E = 1.4426950408889634

F32 = jnp.float32
BF16 = jnp.bfloat16


def _tile(n, pref):
    t = min(n, pref)
    while n % t:
        t //= 2
    return t


def _params(*sem):
    return pltpu.CompilerParams(dimension_semantics=sem, vmem_limit_bytes=VMEM_LIMIT)


def _dot(a, b):
    return jnp.dot(a, b, preferred_element_type=F32)


def _dot_nt(a, b):
    return lax.dot_general(a, b, (((1,), (1,)), ((), ())), preferred_element_type=F32)


def _sigmoid(x):
    return 1.0 / (1.0 + jnp.exp(-x))


def _mod_kernel(c_ref, w_ref, b_ref, o_ref):
    c = c_ref[...]
    ca = (c * _sigmoid(c)).astype(BF16)
    o_ref[0] = _dot(ca, w_ref[0].astype(BF16)) + b_ref[0]


def _modulation(c, w_mod, b_mod):
    nl, d, n = w_mod.shape
    b = c.shape[0]
    bp = 16
    cp = jnp.zeros((bp, d), F32).at[:b].set(c)
    tn = _tile(n, 1024)
    out = pl.pallas_call(
        _mod_kernel,
        grid=(nl, n // tn),
        in_specs=[pl.BlockSpec((bp, d), lambda l, j: (0, 0)),
                  pl.BlockSpec((1, d, tn), lambda l, j: (l, 0, j)),
                  pl.BlockSpec((1, 1, tn), lambda l, j: (l, 0, j))],
        out_specs=pl.BlockSpec((1, bp, tn), lambda l, j: (l, 0, j)),
        out_shape=jax.ShapeDtypeStruct((nl, bp, n), F32),
        compiler_params=_params("arbitrary", "arbitrary"),
    )(cp, w_mod, b_mod.reshape(nl, 1, n))
    return out[:, :b].reshape(nl, b, 6, d)


def _rms(x):
    return x * lax.rsqrt(jnp.mean(x * x, axis=-1, keepdims=True) + NORM_EPS)


def _norm_mod_kernel(h_ref, mod_ref, g_ref, u_ref, *, row):
    y = _rms(h_ref[0]) * g_ref[...]
    u = y * (1.0 + mod_ref[0, row + 1:row + 2, :]) + mod_ref[0, row:row + 1, :]
    u_ref[0] = u.astype(u_ref.dtype)


def _norm_mod(h, mod, g, row):
    b, s, d = h.shape
    ts = _tile(s, 512)
    return pl.pallas_call(
        functools.partial(_norm_mod_kernel, row=row),
        grid=(b, s // ts),
        in_specs=[pl.BlockSpec((1, ts, d), lambda i, j: (i, j, 0)),
                  pl.BlockSpec((1, 6, d), lambda i, j: (i, 0, 0)),
                  pl.BlockSpec((1, d), lambda i, j: (0, 0))],
        out_specs=pl.BlockSpec((1, ts, d), lambda i, j: (i, j, 0)),
        out_shape=jax.ShapeDtypeStruct((b, s, d), BF16),
        compiler_params=_params("arbitrary", "arbitrary"),
    )(h, mod, g.reshape(1, d))


def _final_norm_kernel(h_ref, g_ref, o_ref):
    o_ref[0] = _rms(h_ref[0]) * g_ref[...]


def _final_norm(h, g):
    b, s, d = h.shape
    ts = _tile(s, 512)
    return pl.pallas_call(
        _final_norm_kernel,
        grid=(b, s // ts),
        in_specs=[pl.BlockSpec((1, ts, d), lambda i, j: (i, j, 0)),
                  pl.BlockSpec((1, d), lambda i, j: (0, 0))],
        out_specs=pl.BlockSpec((1, ts, d), lambda i, j: (i, j, 0)),
        out_shape=jax.ShapeDtypeStruct((b, s, d), F32),
        compiler_params=_params("arbitrary", "arbitrary"),
    )(h, g.reshape(1, d))


def _mm_kernel(x_ref, w_ref, b_ref, s_ref, o_ref, *, act):
    acc = (_dot(x_ref[...], w_ref[...]) + b_ref[...]) * s_ref[...]
    if act == "sigmoid":
        acc = _sigmoid(acc)
    o_ref[...] = acc.astype(o_ref.dtype)


def _mm(x, w, bias, scale, out_dtype, act=None):
    m, k = x.shape
    n = w.shape[1]
    tm = _tile(m, 1024)
    tn = n if n <= 1536 else _tile(n, 1024)
    return pl.pallas_call(
        functools.partial(_mm_kernel, act=act),
        grid=(n // tn, m // tm),
        in_specs=[pl.BlockSpec((tm, k), lambda j, i: (i, 0)),
                  pl.BlockSpec((k, tn), lambda j, i: (0, j)),
                  pl.BlockSpec((1, tn), lambda j, i: (0, j)),
                  pl.BlockSpec((1, tn), lambda j, i: (0, j))],
        out_specs=pl.BlockSpec((tm, tn), lambda j, i: (i, j)),
        out_shape=jax.ShapeDtypeStruct((m, n), out_dtype),
        compiler_params=_params("arbitrary", "arbitrary"),
    )(x, w, bias.reshape(1, n), scale.reshape(1, n))


def _mla_prep_kernel(lat_ref, c_ref, s_ref, wq_ref, wqr_ref, wkv_ref, gq_ref, gkv_ref,
                     q_ref, k_ref, v_ref, *, q_lora, kv_lora, n_heads, scale):
    cos = c_ref[...]
    sin = s_ref[...]
    qn = (_rms(lat_ref[:, :q_lora]) * gq_ref[...]).astype(BF16)
    qa = _dot(qn, wq_ref[...])
    qr = _dot(qn, wqr_ref[...])
    kvn = (_rms(lat_ref[:, q_lora:q_lora + kv_lora]) * gkv_ref[...]).astype(BF16)
    kv = _dot(kvn, wkv_ref[...])
    o = q_lora + kv_lora
    k_rope = (lat_ref[:, o:o + LANES] * cos + lat_ref[:, o + LANES:o + 2 * LANES] * sin).astype(BF16)
    for h in range(n_heads):
        a = h * MLA_QK
        q_ref[:, a:a + LANES] = (qa[:, a:a + LANES] * scale).astype(BF16)
        rope = qa[:, a + LANES:a + MLA_QK] * cos + qr[:, h * LANES:(h + 1) * LANES] * sin
        q_ref[:, a + LANES:a + MLA_QK] = (rope * scale).astype(BF16)
        k_ref[:, a:a + LANES] = kv[:, h * LANES:(h + 1) * LANES].astype(BF16)
        k_ref[:, a + LANES:a + MLA_QK] = k_rope
    v_ref[...] = kv[:, n_heads * LANES:].astype(BF16)


def _mla_prep(lat, ctab, stab, wq, wqr, wkv, gq, gkv, n_heads):
    t, nl = lat.shape
    q_lora, kv_lora = gq.shape[0], gkv.shape[0]
    tm = _tile(t, 512)
    kern = functools.partial(_mla_prep_kernel, q_lora=q_lora, kv_lora=kv_lora, n_heads=n_heads,
                             scale=LOG2E * float(QK_NOPE + QK_ROPE) ** -0.5)
    full = lambda a: pl.BlockSpec(a.shape, lambda i: (0,) * a.ndim)
    gq2, gkv2 = gq.reshape(1, -1), gkv.reshape(1, -1)
    return pl.pallas_call(
        kern,
        grid=(t // tm,),
        in_specs=[pl.BlockSpec((tm, nl), lambda i: (i, 0)),
                  pl.BlockSpec((tm, LANES), lambda i: (i, 0)),
                  pl.BlockSpec((tm, LANES), lambda i: (i, 0)),
                  full(wq), full(wqr), full(wkv), full(gq2), full(gkv2)],
        out_specs=[pl.BlockSpec((tm, n_heads * MLA_QK), lambda i: (i, 0)),
                   pl.BlockSpec((tm, n_heads * MLA_QK), lambda i: (i, 0)),
                   pl.BlockSpec((tm, n_heads * HEAD), lambda i: (i, 0))],
        out_shape=[jax.ShapeDtypeStruct((t, n_heads * MLA_QK), BF16),
                   jax.ShapeDtypeStruct((t, n_heads * MLA_QK), BF16),
                   jax.ShapeDtypeStruct((t, n_heads * HEAD), BF16)],
        compiler_params=_params("arbitrary"),
    )(lat, ctab, stab, wq, wqr, wkv, gq2, gkv2)


def _fgate_kernel(f_ref, b_ref, o_ref):
    x = f_ref[...] + b_ref[...]
    v = jnp.minimum(x, 0.0) - jnp.log(1.0 + jnp.exp(-jnp.abs(x)))
    n = v.shape[1]
    lane = lax.broadcasted_iota(jnp.int32, v.shape, 1)
    sh = 1
    while sh < n:
        v = v + jnp.where(lane >= sh, pltpu.roll(v, sh, 1), 0.0)
        sh *= 2
    o_ref[...] = v * LOG2E


def _fgate(f_t, bias):
    r, s = f_t.shape
    return pl.pallas_call(
        _fgate_kernel,
        grid=(1,),
        in_specs=[pl.BlockSpec((r, s), lambda i: (0, 0)), pl.BlockSpec((r, 1), lambda i: (0, 0))],
        out_specs=pl.BlockSpec((r, s), lambda i: (0, 0)),
        out_shape=jax.ShapeDtypeStruct((r, s), F32),
        compiler_params=_params("arbitrary"),
    )(f_t, bias)


def _attn_kernel(*refs, tq, tk, mode, nbuf):
    if mode == "chunk":
        q_ref, k_ref, v_ref, o_ref, vt_ref, m_ref, acc_ref, *bufs = refs
        x_ref = f_ref = None
    elif mode == "decay":
        q_ref, k_ref, v_ref, x_ref, o_ref, vt_ref, m_ref, acc_ref, f_ref, *bufs = refs
    else:
        q_ref, k_ref, v_ref, x_ref, o_ref, vt_ref, m_ref, acc_ref, *bufs = refs
    sbuf, cbuf = bufs[:nbuf], bufs[nbuf:]
    i = pl.program_id(2)
    r = tq // tk

    @pl.when(i == 0)
    def _():
        vt_ref[0:HEAD, :] = v_ref[0].astype(F32).T.astype(BF16)
        vt_ref[HEAD:, :] = jnp.ones((ONES_ROWS, vt_ref.shape[1]), BF16)
        if mode == "decay":
            f_ref[...] = jnp.broadcast_to(x_ref[0], (LANES, x_ref.shape[2])).T

    q = q_ref[0]
    m_ref[...] = jnp.full(m_ref.shape, NEG_INF, F32)
    acc_ref[...] = jnp.zeros(acc_ref.shape, F32)

    def stage_a(j, slot, diag_off=None, bias_idx=None, penalty=None):
        start = pl.multiple_of(j * tk, tk)
        s = _dot_nt(k_ref[0, pl.ds(start, tk), :], q)
        if mode == "decay":
            s = s - jnp.tile(f_ref[pl.ds(start, tk), :], (1, tq // LANES))
        if mode == "band":
            s = s + x_ref[0, bias_idx]
            if penalty is not None:
                s = s + penalty
        elif diag_off is not None:
            row = lax.broadcasted_iota(jnp.int32, (tk, tq), 0) + diag_off
            col = lax.broadcasted_iota(jnp.int32, (tk, tq), 1)
            if mode == "chunk":
                row, col = row // CHUNK, col // CHUNK
            s = jnp.where(row <= col, s, NEG_INF)
        sbuf[slot][...] = s
        cbuf[slot][...] = jnp.max(s, axis=0, keepdims=True)

    def stage_b(j, slot):
        start = pl.multiple_of(j * tk, tk)
        m_prev = m_ref[...]
        m_new = jnp.maximum(m_prev, cbuf[slot][...])
        alpha = jnp.exp2(m_prev - m_new)
        p = jnp.exp2(sbuf[slot][...] - m_new).astype(BF16)
        acc_ref[...] = alpha * acc_ref[...] + _dot(vt_ref[:, pl.ds(start, tk)], p)
        m_ref[...] = m_new

    if mode == "band":
        left = nbuf - r
        first = r * i - left
        blk = [jnp.maximum(first + d, 0) for d in range(nbuf)]
        for d in range(nbuf + BAND_LAG):
            if d < nbuf:
                penalty = jnp.where(first + d < 0, NEG_INF, 0.0).astype(F32) if d < left else None
                stage_a(blk[d], d, bias_idx=d, penalty=penalty)
            if d >= BAND_LAG:
                stage_b(blk[d - BAND_LAG], d - BAND_LAG)
    else:
        stage_a(r * i, 0, diag_off=0)
        stage_a(r * i + 1, 1, diag_off=tk)
        stage_b(r * i, 0)

        def body(p, carry):
            stage_a(2 * p, 0)
            stage_b(jnp.where(p == 0, r * i + 1, 2 * p - 1), 1)
            stage_a(2 * p + 1, 1)
            stage_b(2 * p, 0)
            return carry

        lax.fori_loop(0, i, body, 0)
        stage_b(jnp.where(i == 0, 1, r * i - 1), 1)
    acc = acc_ref[...]
    o_ref[0] = (acc[:HEAD] / acc[HEAD:HEAD + 1]).T.astype(o_ref.dtype)


def _band_bias(rel_bias):
    tq, tk = ATTN_TILES["band"]
    left = LEFT_CHUNKS * CHUNK // tk
    period = tk + tq
    x = jnp.arange(period)
    x = jnp.where(x < tq, x, x - period)
    qpos = jnp.arange(tq)[None, :]
    tabs = []
    for d in range(tq // tk + left):
        rel = (d - left) * tk - x
        vec = rel_bias[:, jnp.clip(rel, -MAX_LEFT, CHUNK - 1) + MAX_LEFT].astype(F32) * LOG2E
        skew = jnp.broadcast_to(vec[:, None, :], (vec.shape[0], tk, period)).reshape(vec.shape[0], tk * period)
        bias = skew[:, :tk * (period - 1)].reshape(vec.shape[0], tk, period - 1)[:, :, :tq]
        kpos = jnp.arange(tk)[:, None] + (d - left) * tk
        dchunk = kpos // CHUNK - qpos // CHUNK
        tabs.append(jnp.where((dchunk <= 0) & (dchunk >= -LEFT_CHUNKS), bias, NEG_INF))
    return jnp.stack(tabs, axis=1)


def _attn(q_arr, k_arr, v_arr, q_off, k_off, v_off, n_heads, dk, mode, extra=None):
    b, s, _ = q_arr.shape
    tq, tk = ATTN_TILES[mode]
    assert s % tq == 0 and tq == 2 * tk
    nbuf = 2
    in_specs = [pl.BlockSpec((1, tq, dk), lambda bi, h, i: (bi, i, q_off + h)),
                pl.BlockSpec((1, s, dk), lambda bi, h, i: (bi, 0, k_off + h)),
                pl.BlockSpec((1, s, HEAD), lambda bi, h, i: (bi, 0, v_off + h))]
    args = [q_arr, k_arr, v_arr]
    scratch = [pltpu.VMEM((HEAD + ONES_ROWS, s), BF16), pltpu.VMEM((1, tq), F32),
               pltpu.VMEM((HEAD + ONES_ROWS, tq), F32)]
    if mode == "decay":
        in_specs.append(pl.BlockSpec((1, 1, s), lambda bi, h, i: (bi * n_heads + h, 0, 0)))
        args.append(extra.reshape(b * n_heads, 1, s))
        scratch.append(pltpu.VMEM((s, LANES), F32))
    if mode == "band":
        nbuf = extra.shape[1]
        in_specs.append(pl.BlockSpec((1, nbuf, tk, tq), lambda bi, h, i: (h, 0, 0, 0)))
        args.append(extra)
    scratch += [pltpu.VMEM((tk, tq), F32)] * nbuf + [pltpu.VMEM((1, tq), F32)] * nbuf
    return pl.pallas_call(
        functools.partial(_attn_kernel, tq=tq, tk=tk, mode=mode, nbuf=nbuf),
        grid=(b, n_heads, s // tq),
        in_specs=in_specs,
        out_specs=pl.BlockSpec((1, tq, HEAD), lambda bi, h, i: (bi, i, h)),
        out_shape=jax.ShapeDtypeStruct((b, s, n_heads * HEAD), BF16),
        scratch_shapes=scratch,
        compiler_params=_params("arbitrary", "arbitrary", "arbitrary"),
    )(*args)


def _merge_kernel(oa_ref, ob_ref, oc_ref, ga_ref, gb_ref, gc_ref, wa_ref, wb_ref, wc_ref, o_ref):
    acc = ga_ref[...].astype(F32) * _dot(oa_ref[...], wa_ref[0])
    acc += gb_ref[...].astype(F32) * _dot(ob_ref[...], wb_ref[0])
    acc += gc_ref[...].astype(F32) * _dot(oc_ref[...], wc_ref[0])
    o_ref[...] = acc.astype(o_ref.dtype)


def _merge(oa, ob, oc, gates, w_branch):
    t, bw = oa.shape
    d = w_branch.shape[2]
    tm, tn = _tile(t, 512), _tile(d, 1024)
    nj = d // tn
    o_spec = pl.BlockSpec((tm, bw), lambda j, i: (i, 0))
    g_spec = lambda g: pl.BlockSpec((tm, tn), lambda j, i: (i, g * nj + j))
    w_spec = lambda g: pl.BlockSpec((1, bw, tn), lambda j, i: (g, 0, j))
    return pl.pallas_call(
        _merge_kernel,
        grid=(nj, t // tm),
        in_specs=[o_spec, o_spec, o_spec, g_spec(0), g_spec(1), g_spec(2), w_spec(0), w_spec(1), w_spec(2)],
        out_specs=pl.BlockSpec((tm, tn), lambda j, i: (i, j)),
        out_shape=jax.ShapeDtypeStruct((t, d), BF16),
        compiler_params=_params("arbitrary", "arbitrary"),
    )(oa, ob, oc, gates, gates, gates, w_branch, w_branch, w_branch)


def _out_proj_kernel(x_ref, w_ref, h_ref, mod_ref, o_ref, *, row):
    o_ref[0] = h_ref[0] + mod_ref[0, row:row + 1, :] * _dot(x_ref[0], w_ref[...])


def _out_proj(x, w, h, mod, row):
    b, s, k = x.shape
    d = w.shape[1]
    tm, tn = _tile(s, 1024), _tile(d, 1024)
    return pl.pallas_call(
        functools.partial(_out_proj_kernel, row=row),
        grid=(d // tn, b, s // tm),
        in_specs=[pl.BlockSpec((1, tm, k), lambda j, bi, i: (bi, i, 0)),
                  pl.BlockSpec((k, tn), lambda j, bi, i: (0, j)),
                  pl.BlockSpec((1, tm, tn), lambda j, bi, i: (bi, i, j)),
                  pl.BlockSpec((1, 6, tn), lambda j, bi, i: (bi, 0, j))],
        out_specs=pl.BlockSpec((1, tm, tn), lambda j, bi, i: (bi, i, j)),
        out_shape=jax.ShapeDtypeStruct((b, s, d), F32),
        compiler_params=_params("arbitrary", "arbitrary", "arbitrary"),
    )(x, w, h, mod)


def _router_kernel(h_ref, mod_ref, g_ref, wr_ref, br_ref, u_ref, idx_ref, wt_ref, rank_ref, cnt_ref, carry, *, row):
    first = jnp.logical_and(pl.program_id(0) == 0, pl.program_id(1) == 0)

    @pl.when(first)
    def _():
        carry[...] = jnp.zeros(carry.shape, F32)

    y = _rms(h_ref[0]) * g_ref[...]
    u = y * (1.0 + mod_ref[0, row + 1:row + 2, :]) + mod_ref[0, row:row + 1, :]
    u_ref[0] = u
    logits = jnp.dot(u, wr_ref[...], preferred_element_type=F32, precision=lax.Precision.HIGHEST) + br_ref[...]
    ts, ne = logits.shape
    lane = lax.broadcasted_iota(jnp.int32, (ts, ne), 1)
    out_lane = lax.broadcasted_iota(jnp.int32, (ts, TOP_K), 1)
    idx = jnp.zeros((ts, TOP_K), jnp.int32)
    val = jnp.zeros((ts, TOP_K), F32)
    onehots = []
    for kk in range(TOP_K):
        m = jnp.max(logits, axis=-1, keepdims=True)
        pick = jnp.min(jnp.where(logits == m, lane, ne), axis=-1, keepdims=True)
        idx = jnp.where(out_lane == kk, pick, idx)
        val = jnp.where(out_lane == kk, m, val)
        onehots.append(lane == pick)
        logits = jnp.where(lane == pick, -jnp.inf, logits)
    e = jnp.exp(val - jnp.max(val, axis=-1, keepdims=True))
    idx_ref[0] = idx
    wt_ref[0] = e / jnp.sum(e, axis=-1, keepdims=True)

    hits = jnp.zeros((ts, ne), F32)
    for oh in onehots:
        hits = hits + oh.astype(F32)
    earlier = lax.broadcasted_iota(jnp.int32, (ts, ts), 1) < lax.broadcasted_iota(jnp.int32, (ts, ts), 0)
    before = _dot(earlier.astype(F32).astype(BF16), hits.astype(BF16)) + carry[...]
    rank = jnp.zeros((ts, TOP_K), F32)
    for kk, oh in enumerate(onehots):
        rank = jnp.where(out_lane == kk, jnp.sum(jnp.where(oh, before, 0.0), axis=-1, keepdims=True), rank)
    rank_ref[0] = rank.astype(jnp.int32)
    carry[...] = carry[...] + jnp.sum(hits, axis=0, keepdims=True)
    cnt_ref[...] = carry[...].astype(jnp.int32)


def _router(h, mod, g, w_router, b_router, row):
    b, s, d = h.shape
    ne = w_router.shape[1]
    ts = _tile(s, 512)
    small = lambda: pl.BlockSpec((1, ts, TOP_K), lambda i, j: (i, j, 0))
    return pl.pallas_call(
        functools.partial(_router_kernel, row=row),
        grid=(b, s // ts),
        in_specs=[pl.BlockSpec((1, ts, d), lambda i, j: (i, j, 0)),
                  pl.BlockSpec((1, 6, d), lambda i, j: (i, 0, 0)),
                  pl.BlockSpec((1, d), lambda i, j: (0, 0)),
                  pl.BlockSpec((d, ne), lambda i, j: (0, 0)),
                  pl.BlockSpec((1, ne), lambda i, j: (0, 0))],
        out_specs=[pl.BlockSpec((1, ts, d), lambda i, j: (i, j, 0)), small(), small(), small(),
                   pl.BlockSpec((1, ne), lambda i, j: (0, 0))],
        out_shape=[jax.ShapeDtypeStruct((b, s, d), F32),
                   jax.ShapeDtypeStruct((b, s, TOP_K), jnp.int32),
                   jax.ShapeDtypeStruct((b, s, TOP_K), F32),
                   jax.ShapeDtypeStruct((b, s, TOP_K), jnp.int32),
                   jax.ShapeDtypeStruct((1, ne), jnp.int32)],
        scratch_shapes=[pltpu.VMEM((1, ne), F32)],
        compiler_params=_params("arbitrary", "arbitrary"),
    )(h, mod, g.reshape(1, d), w_router, b_router.reshape(1, ne))


DMA_UNROLL = 8
DMA_PRIORITIES = 2


def _dispatch_kernel(dest_ref, u_ref, xs_in, xs_out, sem, *, n):
    del xs_in

    def copy(r, dst_row):
        return pltpu.make_async_copy(u_ref.at[lax.shift_right_logical(r, TOP_K_LOG2)], xs_out.at[dst_row], sem)

    def start(pair, c):
        for prio in range(DMA_PRIORITIES):
            r = DMA_PRIORITIES * pair + prio
            copy(r, dest_ref[0, 0, r]).start(priority=prio)
        return c

    def wait(r, c):
        copy(r, 0).wait()
        return c

    lax.fori_loop(0, n // DMA_PRIORITIES, start, 0, unroll=DMA_UNROLL // DMA_PRIORITIES)
    lax.fori_loop(0, n, wait, 0, unroll=DMA_UNROLL)


def _dispatch(u, dest, n_rows):
    t, d = u.shape
    tc = _tile(t, 128)
    nt = t // tc
    n = TOP_K * tc
    return pl.pallas_call(
        functools.partial(_dispatch_kernel, n=n),
        grid=(nt,),
        in_specs=[pl.BlockSpec((1, 1, n), lambda i: (i, 0, 0), memory_space=pltpu.SMEM),
                  pl.BlockSpec((tc, d), lambda i: (i, 0)),
                  pl.BlockSpec(memory_space=pl.ANY)],
        out_specs=pl.BlockSpec(memory_space=pl.ANY),
        out_shape=jax.ShapeDtypeStruct((n_rows, d), F32),
        scratch_shapes=[pltpu.SemaphoreType.DMA(())],
        input_output_aliases={2: 0},
        compiler_params=_params("arbitrary"),
    )(dest.reshape(nt, 1, n), u, jnp.zeros((n_rows, d), F32))


def _expert_changed(be_ref):
    i = pl.program_id(1)
    prev = be_ref[jnp.maximum(i - 1, 0)]
    return jnp.logical_or(i == 0, be_ref[i] != prev)


def _expert_up_kernel(be_ref, nu_ref, x_ref, wg_ref, wl_ref, bg_ref, bl_ref, o_ref, wg_bf, wl_bf):
    i = pl.program_id(1)

    @pl.when(_expert_changed(be_ref))
    def _():
        wg_bf[...] = wg_ref[0, 0].astype(BF16)
        wl_bf[...] = wl_ref[0, 0].astype(BF16)

    @pl.when(i < nu_ref[0])
    def _():
        x = x_ref[...].astype(BF16)
        glu = jnp.minimum(_dot(x, wg_bf[...]) + bg_ref[0, 0], SWIGLU_LIMIT)
        lin = jnp.clip(_dot(x, wl_bf[...]) + bl_ref[0, 0], -SWIGLU_LIMIT, SWIGLU_LIMIT)
        o_ref[...] = (glu * _sigmoid(SWIGLU_ALPHA * glu) * (lin + 1.0)).astype(o_ref.dtype)

    @pl.when(i >= nu_ref[0])
    def _():
        o_ref[...] = jnp.zeros(o_ref.shape, o_ref.dtype)


def _expert_up(xs, w1, b1, layer, block_e, n_used):
    n_rows, d = xs.shape
    nl, ne, _, ff2 = w1.shape
    ff = ff2 // 2
    tn = _tile(ff, 1024)
    nj = ff // tn
    nb = n_rows // MOE_ROWS
    grid_spec = pltpu.PrefetchScalarGridSpec(
        num_scalar_prefetch=2,
        grid=(nj, nb),
        in_specs=[pl.BlockSpec((MOE_ROWS, d), lambda j, i, be, nu: (i, 0)),
                  pl.BlockSpec((1, 1, d, tn), lambda j, i, be, nu: (layer, be[i], 0, j)),
                  pl.BlockSpec((1, 1, d, tn), lambda j, i, be, nu: (layer, be[i], 0, nj + j)),
                  pl.BlockSpec((1, 1, 1, tn), lambda j, i, be, nu: (layer, be[i], 0, j)),
                  pl.BlockSpec((1, 1, 1, tn), lambda j, i, be, nu: (layer, be[i], 0, nj + j))],
        out_specs=pl.BlockSpec((MOE_ROWS, tn), lambda j, i, be, nu: (i, j)),
        scratch_shapes=[pltpu.VMEM((d, tn), BF16), pltpu.VMEM((d, tn), BF16)])
    b1r = b1.reshape(nl, ne, 1, ff2)
    return pl.pallas_call(
        _expert_up_kernel,
        grid_spec=grid_spec,
        out_shape=jax.ShapeDtypeStruct((n_rows, ff), BF16),
        compiler_params=_params("arbitrary", "arbitrary"),
    )(block_e, n_used, xs, w1, w1, b1r, b1r)


def _expert_down_kernel(be_ref, nu_ref, a_ref, w_ref, b_ref, o_ref, w_bf):
    i = pl.program_id(1)

    @pl.when(_expert_changed(be_ref))
    def _():
        w_bf[...] = w_ref[0, 0].astype(BF16)

    @pl.when(i < nu_ref[0])
    def _():
        o_ref[...] = _dot(a_ref[...], w_bf[...]) + b_ref[0, 0]

    @pl.when(i >= nu_ref[0])
    def _():
        o_ref[...] = jnp.zeros(o_ref.shape, o_ref.dtype)


def _expert_down(act, w2, b2, layer, block_e, n_used):
    n_rows, ff = act.shape
    nl, ne, _, d = w2.shape
    tn = _tile(d, 1024)
    nb = n_rows // MOE_ROWS
    grid_spec = pltpu.PrefetchScalarGridSpec(
        num_scalar_prefetch=2,
        grid=(d // tn, nb),
        in_specs=[pl.BlockSpec((MOE_ROWS, ff), lambda j, i, be, nu: (i, 0)),
                  pl.BlockSpec((1, 1, ff, tn), lambda j, i, be, nu: (layer, be[i], 0, j)),
                  pl.BlockSpec((1, 1, 1, tn), lambda j, i, be, nu: (layer, be[i], 0, j))],
        out_specs=pl.BlockSpec((MOE_ROWS, tn), lambda j, i, be, nu: (i, j)),
        scratch_shapes=[pltpu.VMEM((ff, tn), BF16)])
    return pl.pallas_call(
        _expert_down_kernel,
        grid_spec=grid_spec,
        out_shape=jax.ShapeDtypeStruct((n_rows, d), F32),
        compiler_params=_params("arbitrary", "arbitrary"),
    )(block_e, n_used, act, w2, b2.reshape(nl, ne, 1, d))


def _combine_kernel(pos_ref, y_hbm, wt_ref, h_ref, mod_ref, o_ref, buf, sem, *, tc, row):
    n = TOP_K * tc

    def copy(r, src_row):
        return pltpu.make_async_copy(y_hbm.at[src_row], buf.at[r], sem)

    def start(pair, c):
        for prio in range(DMA_PRIORITIES):
            r = DMA_PRIORITIES * pair + prio
            copy(r, pos_ref[0, 0, r]).start(priority=prio)
        return c

    def wait(r, c):
        copy(r, 0).wait()
        return c

    lax.fori_loop(0, n // DMA_PRIORITIES, start, 0, unroll=DMA_UNROLL // DMA_PRIORITIES)
    lax.fori_loop(0, n, wait, 0, unroll=DMA_UNROLL)
    wt = wt_ref[...]
    tot = wt[:, 0:1] * buf[0:tc, :]
    for kk in range(1, TOP_K):
        tot = tot + wt[:, kk:kk + 1] * buf[kk * tc:(kk + 1) * tc, :]
    o_ref[0] = h_ref[0] + mod_ref[0, row:row + 1, :] * tot


def _combine(ys, pos, wts, h, mod, row):
    b, s, d = h.shape
    tc = _tile(s, 128)
    nt = (b * s) // tc
    spb = s // tc
    pos_blk = pos.reshape(nt, tc, TOP_K).transpose(0, 2, 1).reshape(nt, 1, TOP_K * tc)
    return pl.pallas_call(
        functools.partial(_combine_kernel, tc=tc, row=row),
        grid=(nt,),
        in_specs=[pl.BlockSpec((1, 1, TOP_K * tc), lambda i: (i, 0, 0), memory_space=pltpu.SMEM),
                  pl.BlockSpec(memory_space=pl.ANY),
                  pl.BlockSpec((tc, TOP_K), lambda i: (i, 0)),
                  pl.BlockSpec((1, tc, d), lambda i: (i // spb, i % spb, 0)),
                  pl.BlockSpec((1, 6, d), lambda i: (i // spb, 0, 0))],
        out_specs=pl.BlockSpec((1, tc, d), lambda i: (i // spb, i % spb, 0)),
        out_shape=jax.ShapeDtypeStruct((b, s, d), F32),
        scratch_shapes=[pltpu.VMEM((TOP_K * tc, d), F32), pltpu.SemaphoreType.DMA(())],
        compiler_params=_params("arbitrary"),
    )(pos_blk, ys, wts, h, mod)


def _routing_tables(top_idx, rank, counts, n_experts):
    padded = (counts + MOE_ROWS - 1) // MOE_ROWS * MOE_ROWS
    pend = jnp.cumsum(padded)
    pstart = pend - padded
    experts = jnp.arange(n_experts, dtype=jnp.int32)
    dest = rank + jnp.sum(jnp.where(top_idx[..., None] == experts, pstart, 0), axis=-1)
    n_rows = top_idx.size + n_experts * MOE_ROWS
    nb = n_rows // MOE_ROWS
    starts = jnp.arange(nb, dtype=jnp.int32) * MOE_ROWS
    block_e = jnp.minimum(jnp.sum((pend[None, :] <= starts[:, None]).astype(jnp.int32), axis=1), n_experts - 1)
    n_used = (pend[-1:] // MOE_ROWS).astype(jnp.int32)
    return dest.astype(jnp.int32), block_e.astype(jnp.int32), n_used, n_rows


def _moe(h, mod, g, w_router, b_router, w1, b1, w2, b2, layer):
    b, s, d = h.shape
    t = b * s
    ne = w_router.shape[1]
    u, top_idx, top_w, rank, counts = _router(h, mod, g, w_router, b_router, row=3)
    dest, block_e, n_used, n_rows = _routing_tables(top_idx.reshape(t, TOP_K), rank.reshape(t, TOP_K),
                                                    counts.reshape(ne), ne)
    xs = _dispatch(u.reshape(t, d), dest, n_rows)
    act = _expert_up(xs, w1, b1, layer, block_e, n_used)
    ys = _expert_down(act, w2, b2, layer, block_e, n_used)
    return _combine(ys, dest, top_w.reshape(t, TOP_K), h, mod, row=5)


def _rot_cols(w):
    half = w.shape[-1] // 2
    return jnp.concatenate([-w[..., half:], w[..., :half]], axis=-1)


def _pad_cols(w, width):
    return jnp.pad(w, ((0, 0), (0, width - w.shape[1])))


def _mixer_weights(w_in, w_uq, w_ukv, q_lora, kv_lora, n_a, n_b, n_c):
    d = w_in.shape[0]
    o1 = q_lora + kv_lora
    o2 = o1 + QK_ROPE
    o3 = o2 + 3 * n_b * HEAD
    o4 = o3 + 3 * n_c * HEAD
    w_kr = w_in[:, o1:o2]
    w_lat = jnp.concatenate([w_in[:, :o1], _pad_cols(w_kr, LANES), _pad_cols(_rot_cols(w_kr), LANES),
                             _pad_cols(w_in[:, o4:], LANES)], axis=1).astype(BF16)
    w_qkv = w_in[:, o2:o4].astype(BF16)
    qscale = jnp.concatenate([jnp.full((n_b * HEAD,), LOG2E * float(HEAD) ** -0.5, F32), jnp.ones((2 * n_b * HEAD,), F32),
                              jnp.full((n_c * HEAD,), LOG2E * float(HEAD) ** -0.5, F32), jnp.ones((2 * n_c * HEAD,), F32)])
    uq = w_uq.reshape(q_lora, n_a, QK_NOPE + QK_ROPE)
    zeros = jnp.zeros((q_lora, n_a, MLA_QK - QK_NOPE - QK_ROPE), F32)
    wq = jnp.concatenate([uq, zeros], axis=-1).reshape(q_lora, n_a * MLA_QK).astype(BF16)
    wqr = jnp.concatenate([_rot_cols(uq[..., QK_NOPE:]), jnp.zeros((q_lora, n_a, LANES - QK_ROPE), F32)],
                          axis=-1).reshape(q_lora, n_a * LANES).astype(BF16)
    ukv = w_ukv.reshape(kv_lora, n_a, QK_NOPE + HEAD)
    wkv = jnp.concatenate([ukv[..., :QK_NOPE].reshape(kv_lora, -1), ukv[..., QK_NOPE:].reshape(kv_lora, -1)],
                          axis=1).astype(BF16)
    return w_lat, w_qkv, qscale, wq, wqr, wkv


def _rope_tables(positions):
    inv = ROPE_THETA ** (-jnp.arange(0, QK_ROPE, 2, dtype=F32) / QK_ROPE)
    ang = positions.astype(F32).reshape(-1)[:, None] * inv
    pad = jnp.zeros((ang.shape[0], LANES - QK_ROPE), F32)
    cos, sin = jnp.cos(ang), jnp.sin(ang)
    return jnp.concatenate([cos, cos, pad], axis=1), jnp.concatenate([sin, sin, pad], axis=1)


def _mixer(h, mod, ctab, stab, g_norm1, w_in, g_q_lat, w_uq, g_kv_lat, w_ukv, rel_bias, b_forget,
           w_gate, b_gate, w_branch, w_out):
    b, s, d = h.shape
    t = b * s
    q_lora, kv_lora = g_q_lat.shape[0], g_kv_lat.shape[0]
    n_b, n_c = rel_bias.shape[0], b_forget.shape[0]
    n_a = w_ukv.shape[1] // (QK_NOPE + HEAD)
    w_lat, w_qkv, qscale, wq, wqr, wkv = _mixer_weights(w_in, w_uq, w_ukv, q_lora, kv_lora, n_a, n_b, n_c)

    u = _norm_mod(h, mod, g_norm1, row=0).reshape(t, d)
    n_lat = w_lat.shape[1]
    lat = _mm(u, w_lat, jnp.zeros((n_lat,), F32), jnp.ones((n_lat,), F32), F32)
    qkv = _mm(u, w_qkv, jnp.zeros((qscale.shape[0],), F32), qscale, BF16)
    gates = _mm(u, w_gate.astype(BF16), b_gate, jnp.ones_like(b_gate), BF16, act="sigmoid")

    q_cat, k_cat, v_a = _mla_prep(lat, ctab, stab, wq, wqr, wkv, g_q_lat, g_kv_lat, n_a)
    o_a = _attn(q_cat.reshape(b, s, -1), k_cat.reshape(b, s, -1), v_a.reshape(b, s, -1), 0, 0, 0, n_a, MLA_QK, "chunk")
    qkv3 = qkv.reshape(b, s, -1)
    o_b = _attn(qkv3, qkv3, qkv3, 0, n_b, 2 * n_b, n_b, HEAD, "band", _band_bias(rel_bias))
    f_off = q_lora + kv_lora + 2 * LANES
    f_t = lat[:, f_off:f_off + n_c].reshape(b, s, n_c).transpose(0, 2, 1).reshape(b * n_c, s)
    fcum = _fgate(f_t, jnp.tile(b_forget.astype(F32), b).reshape(b * n_c, 1))
    c0 = 3 * n_b
    o_c = _attn(qkv3, qkv3, qkv3, c0, c0 + n_c, c0 + 2 * n_c, n_c, HEAD, "decay", fcum)

    merged = _merge(o_a.reshape(t, -1), o_b.reshape(t, -1), o_c.reshape(t, -1), gates, w_branch.astype(BF16))
    return _out_proj(merged.reshape(b, s, d), w_out.astype(BF16), h, mod, row=2)


def kernel(x, c, positions, w_mod, b_mod, g_norm1, g_norm2, w_in, g_q_lat, w_uq, g_kv_lat, w_ukv, rel_bias, b_forget, w_gate, b_gate, w_branch, w_out, w_router, b_router, w_mlp1, b_mlp1, w_mlp2, b_mlp2, g_final):
    depth = w_mod.shape[0]
    ctab, stab = _rope_tables(positions)
    mods = _modulation(c, w_mod, b_mod)
    h = x
    for l in range(depth):
        h = _mixer(h, mods[l], ctab, stab, g_norm1[l], w_in[l], g_q_lat[l], w_uq[l], g_kv_lat[l], w_ukv[l],
                   rel_bias[l], b_forget[l], w_gate[l], b_gate[l], w_branch[l], w_out[l])
        h = _moe(h, mods[l], g_norm2[l], w_router[l], b_router[l], w_mlp1, b_mlp1, w_mlp2, b_mlp2, l)
    return _final_norm(h, g_final)
```

```python
import functools

import jax
import jax.numpy as jnp
from jax import lax
from jax.experimental import pallas as pl
from jax.experimental.pallas import tpu as pltpu

CHUNK = 64
QK_NOPE = 128
QK_ROPE = 64
HEAD = 128
LEFT_CHUNKS = 8
MAX_LEFT = 128
ROPE_THETA = 10000.0
NEG_INF = -1e30
NORM_EPS = 1e-6
TOP_K = 4
SWIGLU_ALPHA = 1.702
SWIGLU_LIMIT = 7.0

LANES = 128
MLA_QK = 2 * LANES
VMEM_LIMIT = 56 * 1024 * 1024
MOE_ROWS = 256

ATTN_TILES = {"chunk": (1024, 512), "decay": (1024, 512), "band": (512, 256)}
BAND_LAG = 2
ONES_ROWS = 16
LOG2E = 1.4426950408889634

F32 = jnp.float32
BF16 = jnp.bfloat16


def _tile(n, pref):
    t = min(n, pref)
    while n % t:
        t //= 2
    return t


def _params(*sem):
    return pltpu.CompilerParams(dimension_semantics=sem, vmem_limit_bytes=VMEM_LIMIT)


def _dot(a, b):
    return jnp.dot(a, b, preferred_element_type=F32)


def _dot_nt(a, b):
    return lax.dot_general(a, b, (((1,), (1,)), ((), ())), preferred_element_type=F32)


def _sigmoid(x):
    return 1.0 / (1.0 + jnp.exp(-x))


def _mod_kernel(c_ref, w_ref, b_ref, o_ref):
    c = c_ref[...]
    ca = (c * _sigmoid(c)).astype(BF16)
    o_ref[0] = _dot(ca, w_ref[0].astype(BF16)) + b_ref[0]


def _modulation(c, w_mod, b_mod):
    nl, d, n = w_mod.shape
    b = c.shape[0]
    bp = 16
    cp = jnp.zeros((bp, d), F32).at[:b].set(c)
    tn = _tile(n, 1024)
    out = pl.pallas_call(
        _mod_kernel,
        grid=(nl, n // tn),
        in_specs=[pl.BlockSpec((bp, d), lambda l, j: (0, 0)),
                  pl.BlockSpec((1, d, tn), lambda l, j: (l, 0, j)),
                  pl.BlockSpec((1, 1, tn), lambda l, j: (l, 0, j))],
        out_specs=pl.BlockSpec((1, bp, tn), lambda l, j: (l, 0, j)),
        out_shape=jax.ShapeDtypeStruct((nl, bp, n), F32),
        compiler_params=_params("arbitrary", "arbitrary"),
    )(cp, w_mod, b_mod.reshape(nl, 1, n))
    return out[:, :b].reshape(nl, b, 6, d)


def _rms(x):
    return x * lax.rsqrt(jnp.mean(x * x, axis=-1, keepdims=True) + NORM_EPS)


def _norm_mod_kernel(h_ref, mod_ref, g_ref, u_ref, *, row):
    y = _rms(h_ref[0]) * g_ref[...]
    u = y * (1.0 + mod_ref[0, row + 1:row + 2, :]) + mod_ref[0, row:row + 1, :]
    u_ref[0] = u.astype(u_ref.dtype)


def _norm_mod(h, mod, g, row):
    b, s, d = h.shape
    ts = _tile(s, 512)
    return pl.pallas_call(
        functools.partial(_norm_mod_kernel, row=row),
        grid=(b, s // ts),
        in_specs=[pl.BlockSpec((1, ts, d), lambda i, j: (i, j, 0)),
                  pl.BlockSpec((1, 6, d), lambda i, j: (i, 0, 0)),
                  pl.BlockSpec((1, d), lambda i, j: (0, 0))],
        out_specs=pl.BlockSpec((1, ts, d), lambda i, j: (i, j, 0)),
        out_shape=jax.ShapeDtypeStruct((b, s, d), BF16),
        compiler_params=_params("arbitrary", "arbitrary"),
    )(h, mod, g.reshape(1, d))


def _final_norm_kernel(h_ref, g_ref, o_ref):
    o_ref[0] = _rms(h_ref[0]) * g_ref[...]


def _final_norm(h, g):
    b, s, d = h.shape
    ts = _tile(s, 512)
    return pl.pallas_call(
        _final_norm_kernel,
        grid=(b, s // ts),
        in_specs=[pl.BlockSpec((1, ts, d), lambda i, j: (i, j, 0)),
                  pl.BlockSpec((1, d), lambda i, j: (0, 0))],
        out_specs=pl.BlockSpec((1, ts, d), lambda i, j: (i, j, 0)),
        out_shape=jax.ShapeDtypeStruct((b, s, d), F32),
        compiler_params=_params("arbitrary", "arbitrary"),
    )(h, g.reshape(1, d))


def _mm_kernel(x_ref, w_ref, b_ref, s_ref, o_ref, *, act):
    acc = (_dot(x_ref[...], w_ref[...]) + b_ref[...]) * s_ref[...]
    if act == "sigmoid":
        acc = _sigmoid(acc)
    o_ref[...] = acc.astype(o_ref.dtype)


def _mm(x, w, bias, scale, out_dtype, act=None):
    m, k = x.shape
    n = w.shape[1]
    tm = _tile(m, 1024)
    tn = n if n <= 1536 else _tile(n, 1024)
    return pl.pallas_call(
        functools.partial(_mm_kernel, act=act),
        grid=(n // tn, m // tm),
        in_specs=[pl.BlockSpec((tm, k), lambda j, i: (i, 0)),
                  pl.BlockSpec((k, tn), lambda j, i: (0, j)),
                  pl.BlockSpec((1, tn), lambda j, i: (0, j)),
                  pl.BlockSpec((1, tn), lambda j, i: (0, j))],
        out_specs=pl.BlockSpec((tm, tn), lambda j, i: (i, j)),
        out_shape=jax.ShapeDtypeStruct((m, n), out_dtype),
        compiler_params=_params("arbitrary", "arbitrary"),
    )(x, w, bias.reshape(1, n), scale.reshape(1, n))


def _mla_prep_kernel(lat_ref, c_ref, s_ref, wq_ref, wqr_ref, wkv_ref, gq_ref, gkv_ref,
                     q_ref, k_ref, v_ref, *, q_lora, kv_lora, n_heads, scale):
    cos = c_ref[...]
    sin = s_ref[...]
    qn = (_rms(lat_ref[:, :q_lora]) * gq_ref[...]).astype(BF16)
    qa = _dot(qn, wq_ref[...])
    qr = _dot(qn, wqr_ref[...])
    kvn = (_rms(lat_ref[:, q_lora:q_lora + kv_lora]) * gkv_ref[...]).astype(BF16)
    kv = _dot(kvn, wkv_ref[...])
    o = q_lora + kv_lora
    k_rope = (lat_ref[:, o:o + LANES] * cos + lat_ref[:, o + LANES:o + 2 * LANES] * sin).astype(BF16)
    for h in range(n_heads):
        a = h * MLA_QK
        q_ref[:, a:a + LANES] = (qa[:, a:a + LANES] * scale).astype(BF16)
        rope = qa[:, a + LANES:a + MLA_QK] * cos + qr[:, h * LANES:(h + 1) * LANES] * sin
        q_ref[:, a + LANES:a + MLA_QK] = (rope * scale).astype(BF16)
        k_ref[:, a:a + LANES] = kv[:, h * LANES:(h + 1) * LANES].astype(BF16)
        k_ref[:, a + LANES:a + MLA_QK] = k_rope
    v_ref[...] = kv[:, n_heads * LANES:].astype(BF16)


def _mla_prep(lat, ctab, stab, wq, wqr, wkv, gq, gkv, n_heads):
    t, nl = lat.shape
    q_lora, kv_lora = gq.shape[0], gkv.shape[0]
    tm = _tile(t, 512)
    kern = functools.partial(_mla_prep_kernel, q_lora=q_lora, kv_lora=kv_lora, n_heads=n_heads,
                             scale=LOG2E * float(QK_NOPE + QK_ROPE) ** -0.5)
    full = lambda a: pl.BlockSpec(a.shape, lambda i: (0,) * a.ndim)
    gq2, gkv2 = gq.reshape(1, -1), gkv.reshape(1, -1)
    return pl.pallas_call(
        kern,
        grid=(t // tm,),
        in_specs=[pl.BlockSpec((tm, nl), lambda i: (i, 0)),
                  pl.BlockSpec((tm, LANES), lambda i: (i, 0)),
                  pl.BlockSpec((tm, LANES), lambda i: (i, 0)),
                  full(wq), full(wqr), full(wkv), full(gq2), full(gkv2)],
        out_specs=[pl.BlockSpec((tm, n_heads * MLA_QK), lambda i: (i, 0)),
                   pl.BlockSpec((tm, n_heads * MLA_QK), lambda i: (i, 0)),
                   pl.BlockSpec((tm, n_heads * HEAD), lambda i: (i, 0))],
        out_shape=[jax.ShapeDtypeStruct((t, n_heads * MLA_QK), BF16),
                   jax.ShapeDtypeStruct((t, n_heads * MLA_QK), BF16),
                   jax.ShapeDtypeStruct((t, n_heads * HEAD), BF16)],
        compiler_params=_params("arbitrary"),
    )(lat, ctab, stab, wq, wqr, wkv, gq2, gkv2)


def _fgate_kernel(f_ref, b_ref, o_ref):
    x = f_ref[...] + b_ref[...]
    v = jnp.minimum(x, 0.0) - jnp.log(1.0 + jnp.exp(-jnp.abs(x)))
    n = v.shape[1]
    lane = lax.broadcasted_iota(jnp.int32, v.shape, 1)
    sh = 1
    while sh < n:
        v = v + jnp.where(lane >= sh, pltpu.roll(v, sh, 1), 0.0)
        sh *= 2
    o_ref[...] = v * LOG2E


def _fgate(f_t, bias):
    r, s = f_t.shape
    return pl.pallas_call(
        _fgate_kernel,
        grid=(1,),
        in_specs=[pl.BlockSpec((r, s), lambda i: (0, 0)), pl.BlockSpec((r, 1), lambda i: (0, 0))],
        out_specs=pl.BlockSpec((r, s), lambda i: (0, 0)),
        out_shape=jax.ShapeDtypeStruct((r, s), F32),
        compiler_params=_params("arbitrary"),
    )(f_t, bias)


def _attn_kernel(*refs, tq, tk, mode, nbuf):
    if mode == "chunk":
        q_ref, k_ref, v_ref, o_ref, vt_ref, m_ref, acc_ref, *bufs = refs
        x_ref = f_ref = None
    elif mode == "decay":
        q_ref, k_ref, v_ref, x_ref, o_ref, vt_ref, m_ref, acc_ref, f_ref, *bufs = refs
    else:
        q_ref, k_ref, v_ref, x_ref, o_ref, vt_ref, m_ref, acc_ref, *bufs = refs
    sbuf, cbuf = bufs[:nbuf], bufs[nbuf:]
    i = pl.program_id(2)
    r = tq // tk

    @pl.when(i == 0)
    def _():
        vt_ref[0:HEAD, :] = v_ref[0].astype(F32).T.astype(BF16)
        vt_ref[HEAD:, :] = jnp.ones((ONES_ROWS, vt_ref.shape[1]), BF16)
        if mode == "decay":
            f_ref[...] = jnp.broadcast_to(x_ref[0], (LANES, x_ref.shape[2])).T

    q = q_ref[0]
    m_ref[...] = jnp.full(m_ref.shape, NEG_INF, F32)
    acc_ref[...] = jnp.zeros(acc_ref.shape, F32)

    def stage_a(j, slot, diag_off=None, bias_idx=None, penalty=None):
        start = pl.multiple_of(j * tk, tk)
        s = _dot_nt(k_ref[0, pl.ds(start, tk), :], q)
        if mode == "decay":
            s = s - jnp.tile(f_ref[pl.ds(start, tk), :], (1, tq // LANES))
        if mode == "band":
            s = s + x_ref[0, bias_idx]
            if penalty is not None:
                s = s + penalty
        elif diag_off is not None:
            row = lax.broadcasted_iota(jnp.int32, (tk, tq), 0) + diag_off
            col = lax.broadcasted_iota(jnp.int32, (tk, tq), 1)
            if mode == "chunk":
                row, col = row // CHUNK, col // CHUNK
            s = jnp.where(row <= col, s, NEG_INF)
        sbuf[slot][...] = s
        cbuf[slot][...] = jnp.max(s, axis=0, keepdims=True)

    def stage_b(j, slot):
        start = pl.multiple_of(j * tk, tk)
        m_prev = m_ref[...]
        m_new = jnp.maximum(m_prev, cbuf[slot][...])
        alpha = jnp.exp2(m_prev - m_new)
        p = jnp.exp2(sbuf[slot][...] - m_new).astype(BF16)
        acc_ref[...] = alpha * acc_ref[...] + _dot(vt_ref[:, pl.ds(start, tk)], p)
        m_ref[...] = m_new

    if mode == "band":
        left = nbuf - r
        first = r * i - left
        blk = [jnp.maximum(first + d, 0) for d in range(nbuf)]
        for d in range(nbuf + BAND_LAG):
            if d < nbuf:
                penalty = jnp.where(first + d < 0, NEG_INF, 0.0).astype(F32) if d < left else None
                stage_a(blk[d], d, bias_idx=d, penalty=penalty)
            if d >= BAND_LAG:
                stage_b(blk[d - BAND_LAG], d - BAND_LAG)
    else:
        stage_a(r * i, 0, diag_off=0)
        stage_a(r * i + 1, 1, diag_off=tk)
        stage_b(r * i, 0)

        def body(p, carry):
            stage_a(2 * p, 0)
            stage_b(jnp.where(p == 0, r * i + 1, 2 * p - 1), 1)
            stage_a(2 * p + 1, 1)
            stage_b(2 * p, 0)
            return carry

        lax.fori_loop(0, i, body, 0)
        stage_b(jnp.where(i == 0, 1, r * i - 1), 1)
    acc = acc_ref[...]
    o_ref[0] = (acc[:HEAD] / acc[HEAD:HEAD + 1]).T.astype(o_ref.dtype)


def _band_bias(rel_bias):
    tq, tk = ATTN_TILES["band"]
    left = LEFT_CHUNKS * CHUNK // tk
    period = tk + tq
    x = jnp.arange(period)
    x = jnp.where(x < tq, x, x - period)
    qpos = jnp.arange(tq)[None, :]
    tabs = []
    for d in range(tq // tk + left):
        rel = (d - left) * tk - x
        vec = rel_bias[:, jnp.clip(rel, -MAX_LEFT, CHUNK - 1) + MAX_LEFT].astype(F32) * LOG2E
        skew = jnp.broadcast_to(vec[:, None, :], (vec.shape[0], tk, period)).reshape(vec.shape[0], tk * period)
        bias = skew[:, :tk * (period - 1)].reshape(vec.shape[0], tk, period - 1)[:, :, :tq]
        kpos = jnp.arange(tk)[:, None] + (d - left) * tk
        dchunk = kpos // CHUNK - qpos // CHUNK
        tabs.append(jnp.where((dchunk <= 0) & (dchunk >= -LEFT_CHUNKS), bias, NEG_INF))
    return jnp.stack(tabs, axis=1)


def _attn(q_arr, k_arr, v_arr, q_off, k_off, v_off, n_heads, dk, mode, extra=None):
    b, s, _ = q_arr.shape
    tq, tk = ATTN_TILES[mode]
    assert s % tq == 0 and tq == 2 * tk
    nbuf = 2
    in_specs = [pl.BlockSpec((1, tq, dk), lambda bi, h, i: (bi, i, q_off + h)),
                pl.BlockSpec((1, s, dk), lambda bi, h, i: (bi, 0, k_off + h)),
                pl.BlockSpec((1, s, HEAD), lambda bi, h, i: (bi, 0, v_off + h))]
    args = [q_arr, k_arr, v_arr]
    scratch = [pltpu.VMEM((HEAD + ONES_ROWS, s), BF16), pltpu.VMEM((1, tq), F32),
               pltpu.VMEM((HEAD + ONES_ROWS, tq), F32)]
    if mode == "decay":
        in_specs.append(pl.BlockSpec((1, 1, s), lambda bi, h, i: (bi * n_heads + h, 0, 0)))
        args.append(extra.reshape(b * n_heads, 1, s))
        scratch.append(pltpu.VMEM((s, LANES), F32))
    if mode == "band":
        nbuf = extra.shape[1]
        in_specs.append(pl.BlockSpec((1, nbuf, tk, tq), lambda bi, h, i: (h, 0, 0, 0)))
        args.append(extra)
    scratch += [pltpu.VMEM((tk, tq), F32)] * nbuf + [pltpu.VMEM((1, tq), F32)] * nbuf
    return pl.pallas_call(
        functools.partial(_attn_kernel, tq=tq, tk=tk, mode=mode, nbuf=nbuf),
        grid=(b, n_heads, s // tq),
        in_specs=in_specs,
        out_specs=pl.BlockSpec((1, tq, HEAD), lambda bi, h, i: (bi, i, h)),
        out_shape=jax.ShapeDtypeStruct((b, s, n_heads * HEAD), BF16),
        scratch_shapes=scratch,
        compiler_params=_params("arbitrary", "arbitrary", "arbitrary"),
    )(*args)


def _merge_kernel(oa_ref, ob_ref, oc_ref, ga_ref, gb_ref, gc_ref, wa_ref, wb_ref, wc_ref, o_ref):
    acc = ga_ref[...].astype(F32) * _dot(oa_ref[...], wa_ref[0])
    acc += gb_ref[...].astype(F32) * _dot(ob_ref[...], wb_ref[0])
    acc += gc_ref[...].astype(F32) * _dot(oc_ref[...], wc_ref[0])
    o_ref[...] = acc.astype(o_ref.dtype)


def _merge(oa, ob, oc, gates, w_branch):
    t, bw = oa.shape
    d = w_branch.shape[2]
    tm, tn = _tile(t, 512), _tile(d, 1024)
    nj = d // tn
    o_spec = pl.BlockSpec((tm, bw), lambda j, i: (i, 0))
    g_spec = lambda g: pl.BlockSpec((tm, tn), lambda j, i: (i, g * nj + j))
    w_spec = lambda g: pl.BlockSpec((1, bw, tn), lambda j, i: (g, 0, j))
    return pl.pallas_call(
        _merge_kernel,
        grid=(nj, t // tm),
        in_specs=[o_spec, o_spec, o_spec, g_spec(0), g_spec(1), g_spec(2), w_spec(0), w_spec(1), w_spec(2)],
        out_specs=pl.BlockSpec((tm, tn), lambda j, i: (i, j)),
        out_shape=jax.ShapeDtypeStruct((t, d), BF16),
        compiler_params=_params("arbitrary", "arbitrary"),
    )(oa, ob, oc, gates, gates, gates, w_branch, w_branch, w_branch)


def _out_proj_kernel(x_ref, w_ref, h_ref, mod_ref, o_ref, *, row):
    o_ref[0] = h_ref[0] + mod_ref[0, row:row + 1, :] * _dot(x_ref[0], w_ref[...])


def _out_proj(x, w, h, mod, row):
    b, s, k = x.shape
    d = w.shape[1]
    tm, tn = _tile(s, 1024), _tile(d, 1024)
    return pl.pallas_call(
        functools.partial(_out_proj_kernel, row=row),
        grid=(d // tn, b, s // tm),
        in_specs=[pl.BlockSpec((1, tm, k), lambda j, bi, i: (bi, i, 0)),
                  pl.BlockSpec((k, tn), lambda j, bi, i: (0, j)),
                  pl.BlockSpec((1, tm, tn), lambda j, bi, i: (bi, i, j)),
                  pl.BlockSpec((1, 6, tn), lambda j, bi, i: (bi, 0, j))],
        out_specs=pl.BlockSpec((1, tm, tn), lambda j, bi, i: (bi, i, j)),
        out_shape=jax.ShapeDtypeStruct((b, s, d), F32),
        compiler_params=_params("arbitrary", "arbitrary", "arbitrary"),
    )(x, w, h, mod)


def _router_kernel(h_ref, mod_ref, g_ref, wr_ref, br_ref, u_ref, idx_ref, wt_ref, rank_ref, cnt_ref, carry, *, row):
    first = jnp.logical_and(pl.program_id(0) == 0, pl.program_id(1) == 0)

    @pl.when(first)
    def _():
        carry[...] = jnp.zeros(carry.shape, F32)

    y = _rms(h_ref[0]) * g_ref[...]
    u = y * (1.0 + mod_ref[0, row + 1:row + 2, :]) + mod_ref[0, row:row + 1, :]
    u_ref[0] = u
    logits = jnp.dot(u, wr_ref[...], preferred_element_type=F32, precision=lax.Precision.HIGHEST) + br_ref[...]
    ts, ne = logits.shape
    lane = lax.broadcasted_iota(jnp.int32, (ts, ne), 1)
    out_lane = lax.broadcasted_iota(jnp.int32, (ts, TOP_K), 1)
    idx = jnp.zeros((ts, TOP_K), jnp.int32)
    val = jnp.zeros((ts, TOP_K), F32)
    onehots = []
    for kk in range(TOP_K):
        m = jnp.max(logits, axis=-1, keepdims=True)
        pick = jnp.min(jnp.where(logits == m, lane, ne), axis=-1, keepdims=True)
        idx = jnp.where(out_lane == kk, pick, idx)
        val = jnp.where(out_lane == kk, m, val)
        onehots.append(lane == pick)
        logits = jnp.where(lane == pick, -jnp.inf, logits)
    e = jnp.exp(val - jnp.max(val, axis=-1, keepdims=True))
    idx_ref[0] = idx
    wt_ref[0] = e / jnp.sum(e, axis=-1, keepdims=True)

    hits = jnp.zeros((ts, ne), F32)
    for oh in onehots:
        hits = hits + oh.astype(F32)
    earlier = lax.broadcasted_iota(jnp.int32, (ts, ts), 1) < lax.broadcasted_iota(jnp.int32, (ts, ts), 0)
    before = _dot(earlier.astype(F32).astype(BF16), hits.astype(BF16)) + carry[...]
    rank = jnp.zeros((ts, TOP_K), F32)
    for kk, oh in enumerate(onehots):
        rank = jnp.where(out_lane == kk, jnp.sum(jnp.where(oh, before, 0.0), axis=-1, keepdims=True), rank)
    rank_ref[0] = rank.astype(jnp.int32)
    carry[...] = carry[...] + jnp.sum(hits, axis=0, keepdims=True)
    cnt_ref[...] = carry[...].astype(jnp.int32)


def _router(h, mod, g, w_router, b_router, row):
    b, s, d = h.shape
    ne = w_router.shape[1]
    ts = _tile(s, 512)
    small = lambda: pl.BlockSpec((1, ts, TOP_K), lambda i, j: (i, j, 0))
    return pl.pallas_call(
        functools.partial(_router_kernel, row=row),
        grid=(b, s // ts),
        in_specs=[pl.BlockSpec((1, ts, d), lambda i, j: (i, j, 0)),
                  pl.BlockSpec((1, 6, d), lambda i, j: (i, 0, 0)),
                  pl.BlockSpec((1, d), lambda i, j: (0, 0)),
                  pl.BlockSpec((d, ne), lambda i, j: (0, 0)),
                  pl.BlockSpec((1, ne), lambda i, j: (0, 0))],
        out_specs=[pl.BlockSpec((1, ts, d), lambda i, j: (i, j, 0)), small(), small(), small(),
                   pl.BlockSpec((1, ne), lambda i, j: (0, 0))],
        out_shape=[jax.ShapeDtypeStruct((b, s, d), F32),
                   jax.ShapeDtypeStruct((b, s, TOP_K), jnp.int32),
                   jax.ShapeDtypeStruct((b, s, TOP_K), F32),
                   jax.ShapeDtypeStruct((b, s, TOP_K), jnp.int32),
                   jax.ShapeDtypeStruct((1, ne), jnp.int32)],
        scratch_shapes=[pltpu.VMEM((1, ne), F32)],
        compiler_params=_params("arbitrary", "arbitrary"),
    )(h, mod, g.reshape(1, d), w_router, b_router.reshape(1, ne))


DMA_PRIORITIES = 2


def _dispatch_kernel(dest_ref, u_ref, xs_in, xs_out, sem, *, n):
    del xs_in

    def copy(r, dst_row):
        return pltpu.make_async_copy(u_ref.at[r // TOP_K], xs_out.at[dst_row], sem)

    for r in range(n):
        copy(r, dest_ref[0, 0, r]).start(priority=r % DMA_PRIORITIES)
    for r in range(n):
        copy(r, 0).wait()


def _dispatch(u, dest, xs_init):
    t, d = u.shape
    n_rows = xs_init.shape[0]
    tc = _tile(t, 128)
    nt = t // tc
    n = TOP_K * tc
    return pl.pallas_call(
        functools.partial(_dispatch_kernel, n=n),
        grid=(nt,),
        in_specs=[pl.BlockSpec((1, 1, n), lambda i: (i, 0, 0), memory_space=pltpu.SMEM),
                  pl.BlockSpec((tc, d), lambda i: (i, 0)),
                  pl.BlockSpec(memory_space=pl.ANY)],
        out_specs=pl.BlockSpec(memory_space=pl.ANY),
        out_shape=jax.ShapeDtypeStruct((n_rows, d), F32),
        scratch_shapes=[pltpu.SemaphoreType.DMA(())],
        input_output_aliases={2: 0},
        compiler_params=_params("arbitrary"),
    )(dest.reshape(nt, 1, n), u, xs_init)


def _group_start(be_ref, nu_ref):
    i = pl.program_id(1)
    prev = be_ref[jnp.maximum(i - 1, 0)]
    return jnp.logical_and(i < nu_ref[0], jnp.logical_or(i == 0, be_ref[i] != prev))


def _stage_weights(be_ref, nx_ref, nu_ref, fetch, landing, staged):
    i = pl.program_id(1)

    @pl.when(_group_start(be_ref, nu_ref))
    def _():
        @pl.when(i == 0)
        def _():
            for c in fetch(be_ref[i]):
                c.start()

        for c in fetch(be_ref[i]):
            c.wait()
        for src, dst in zip(landing, staged):
            dst[...] = src[...].astype(BF16)

        @pl.when(nx_ref[i] >= 0)
        def _():
            for c in fetch(nx_ref[i]):
                c.start()


def _expert_up_kernel(be_ref, nx_ref, nu_ref, x_ref, w_hbm, bg_ref, bl_ref, o_ref, wg_f, wl_f, wg_bf, wl_bf, sem,
                      *, layer, tn, ff):
    i = pl.program_id(1)
    col = pl.multiple_of(pl.program_id(0) * tn, tn)

    def fetch(e):
        return [pltpu.make_async_copy(w_hbm.at[layer, e, :, pl.ds(col, tn)], wg_f, sem.at[0]),
                pltpu.make_async_copy(w_hbm.at[layer, e, :, pl.ds(ff + col, tn)], wl_f, sem.at[1])]

    _stage_weights(be_ref, nx_ref, nu_ref, fetch, (wg_f, wl_f), (wg_bf, wl_bf))

    @pl.when(i < nu_ref[0])
    def _():
        x = x_ref[...].astype(BF16)
        glu = jnp.minimum(_dot(x, wg_bf[...]) + bg_ref[0, 0], SWIGLU_LIMIT)
        lin = jnp.clip(_dot(x, wl_bf[...]) + bl_ref[0, 0], -SWIGLU_LIMIT, SWIGLU_LIMIT)
        o_ref[...] = (glu * _sigmoid(SWIGLU_ALPHA * glu) * (lin + 1.0)).astype(o_ref.dtype)

    @pl.when(i >= nu_ref[0])
    def _():
        o_ref[...] = jnp.zeros(o_ref.shape, o_ref.dtype)


def _expert_up(xs, w1, b1, layer, block_e, next_e, n_used):
    n_rows, d = xs.shape
    nl, ne, _, ff2 = w1.shape
    ff = ff2 // 2
    tn = _tile(ff, 1024)
    nj = ff // tn
    nb = n_rows // MOE_ROWS
    grid_spec = pltpu.PrefetchScalarGridSpec(
        num_scalar_prefetch=3,
        grid=(nj, nb),
        in_specs=[pl.BlockSpec((MOE_ROWS, d), lambda j, i, be, nx, nu: (i, 0)),
                  pl.BlockSpec(memory_space=pl.ANY),
                  pl.BlockSpec((1, 1, 1, tn), lambda j, i, be, nx, nu: (layer, be[i], 0, j)),
                  pl.BlockSpec((1, 1, 1, tn), lambda j, i, be, nx, nu: (layer, be[i], 0, nj + j))],
        out_specs=pl.BlockSpec((MOE_ROWS, tn), lambda j, i, be, nx, nu: (i, j)),
        scratch_shapes=[pltpu.VMEM((d, tn), F32), pltpu.VMEM((d, tn), F32),
                        pltpu.VMEM((d, tn), BF16), pltpu.VMEM((d, tn), BF16), pltpu.SemaphoreType.DMA((2,))])
    b1r = b1.reshape(nl, ne, 1, ff2)
    return pl.pallas_call(
        functools.partial(_expert_up_kernel, layer=layer, tn=tn, ff=ff),
        grid_spec=grid_spec,
        out_shape=jax.ShapeDtypeStruct((n_rows, ff), BF16),
        compiler_params=_params("arbitrary", "arbitrary"),
    )(block_e, next_e, n_used, xs, w1, b1r, b1r)


def _expert_down_kernel(be_ref, nx_ref, nu_ref, a_ref, w_hbm, b_ref, o_ref, w_f, w_bf, sem, *, layer, tn):
    i = pl.program_id(1)
    col = pl.multiple_of(pl.program_id(0) * tn, tn)

    def fetch(e):
        return [pltpu.make_async_copy(w_hbm.at[layer, e, :, pl.ds(col, tn)], w_f, sem.at[0])]

    _stage_weights(be_ref, nx_ref, nu_ref, fetch, (w_f,), (w_bf,))

    @pl.when(i < nu_ref[0])
    def _():
        o_ref[...] = _dot(a_ref[...], w_bf[...]) + b_ref[0, 0]

    @pl.when(i >= nu_ref[0])
    def _():
        o_ref[...] = jnp.zeros(o_ref.shape, o_ref.dtype)


def _expert_down(act, w2, b2, layer, block_e, next_e, n_used):
    n_rows, ff = act.shape
    nl, ne, _, d = w2.shape
    tn = _tile(d, 1024)
    nb = n_rows // MOE_ROWS
    grid_spec = pltpu.PrefetchScalarGridSpec(
        num_scalar_prefetch=3,
        grid=(d // tn, nb),
        in_specs=[pl.BlockSpec((MOE_ROWS, ff), lambda j, i, be, nx, nu: (i, 0)),
                  pl.BlockSpec(memory_space=pl.ANY),
                  pl.BlockSpec((1, 1, 1, tn), lambda j, i, be, nx, nu: (layer, be[i], 0, j))],
        out_specs=pl.BlockSpec((MOE_ROWS, tn), lambda j, i, be, nx, nu: (i, j)),
        scratch_shapes=[pltpu.VMEM((ff, tn), F32), pltpu.VMEM((ff, tn), BF16), pltpu.SemaphoreType.DMA((1,))])
    return pl.pallas_call(
        functools.partial(_expert_down_kernel, layer=layer, tn=tn),
        grid_spec=grid_spec,
        out_shape=jax.ShapeDtypeStruct((n_rows, d), F32),
        compiler_params=_params("arbitrary", "arbitrary"),
    )(block_e, next_e, n_used, act, w2, b2.reshape(nl, ne, 1, d))


def _combine_kernel(pos_ref, y_hbm, wt_ref, h_ref, mod_ref, o_ref, buf, sem, *, tc, row):
    n = TOP_K * tc

    def copy(r, src_row):
        return pltpu.make_async_copy(y_hbm.at[src_row], buf.at[r], sem)

    for r in range(n):
        copy(r, pos_ref[0, 0, r]).start(priority=r % DMA_PRIORITIES)
    for r in range(n):
        copy(r, 0).wait()
    wt = wt_ref[...]
    tot = wt[:, 0:1] * buf[0:tc, :]
    for kk in range(1, TOP_K):
        tot = tot + wt[:, kk:kk + 1] * buf[kk * tc:(kk + 1) * tc, :]
    o_ref[0] = h_ref[0] + mod_ref[0, row:row + 1, :] * tot


def _combine(ys, pos, wts, h, mod, row):
    b, s, d = h.shape
    tc = _tile(s, 128)
    nt = (b * s) // tc
    spb = s // tc
    pos_blk = pos.reshape(nt, tc, TOP_K).transpose(0, 2, 1).reshape(nt, 1, TOP_K * tc)
    return pl.pallas_call(
        functools.partial(_combine_kernel, tc=tc, row=row),
        grid=(nt,),
        in_specs=[pl.BlockSpec((1, 1, TOP_K * tc), lambda i: (i, 0, 0), memory_space=pltpu.SMEM),
                  pl.BlockSpec(memory_space=pl.ANY),
                  pl.BlockSpec((tc, TOP_K), lambda i: (i, 0)),
                  pl.BlockSpec((1, tc, d), lambda i: (i // spb, i % spb, 0)),
                  pl.BlockSpec((1, 6, d), lambda i: (i // spb, 0, 0))],
        out_specs=pl.BlockSpec((1, tc, d), lambda i: (i // spb, i % spb, 0)),
        out_shape=jax.ShapeDtypeStruct((b, s, d), F32),
        scratch_shapes=[pltpu.VMEM((TOP_K * tc, d), F32), pltpu.SemaphoreType.DMA(())],
        compiler_params=_params("arbitrary"),
    )(pos_blk, ys, wts, h, mod)


def _routing_tables(top_idx, rank, counts, n_experts):
    padded = (counts + MOE_ROWS - 1) // MOE_ROWS * MOE_ROWS
    pend = jnp.cumsum(padded)
    pstart = pend - padded
    experts = jnp.arange(n_experts, dtype=jnp.int32)
    dest = rank + jnp.sum(jnp.where(top_idx[..., None] == experts, pstart, 0), axis=-1)
    n_rows = top_idx.size + n_experts * MOE_ROWS
    nb = n_rows // MOE_ROWS
    starts = jnp.arange(nb, dtype=jnp.int32) * MOE_ROWS
    block_e = jnp.minimum(jnp.sum((pend[None, :] <= starts[:, None]).astype(jnp.int32), axis=1), n_experts - 1)
    n_used = (pend[-1:] // MOE_ROWS).astype(jnp.int32)
    blocks = jnp.arange(nb, dtype=jnp.int32)
    prev_e = jnp.concatenate([block_e[:1] - 1, block_e[:-1]])
    first_of_group = jnp.where((block_e != prev_e) & (blocks < n_used[0]), blocks, nb)
    later = jnp.concatenate([lax.cummin(first_of_group, reverse=True)[1:], jnp.full((1,), nb, jnp.int32)])
    next_e = jnp.where(later < nb, block_e[jnp.minimum(later, nb - 1)], -1)
    return dest.astype(jnp.int32), block_e.astype(jnp.int32), next_e.astype(jnp.int32), n_used, n_rows


def _moe(h, mod, g, w_router, b_router, w1, b1, w2, b2, layer, xs_prev):
    b, s, d = h.shape
    t = b * s
    ne = w_router.shape[1]
    u, top_idx, top_w, rank, counts = _router(h, mod, g, w_router, b_router, row=3)
    dest, block_e, next_e, n_used, n_rows = _routing_tables(top_idx.reshape(t, TOP_K), rank.reshape(t, TOP_K),
                                                            counts.reshape(ne), ne)
    xs = _dispatch(u.reshape(t, d), dest, jnp.zeros((n_rows, d), F32) if xs_prev is None else xs_prev)
    act = _expert_up(xs, w1, b1, layer, block_e, next_e, n_used)
    ys = _expert_down(act, w2, b2, layer, block_e, next_e, n_used)
    return _combine(ys, dest, top_w.reshape(t, TOP_K), h, mod, row=5), xs


def _rot_cols(w):
    half = w.shape[-1] // 2
    return jnp.concatenate([-w[..., half:], w[..., :half]], axis=-1)


def _pad_cols(w, width):
    return jnp.pad(w, ((0, 0), (0, width - w.shape[1])))


def _mixer_weights(w_in, w_uq, w_ukv, q_lora, kv_lora, n_a, n_b, n_c):
    d = w_in.shape[0]
    o1 = q_lora + kv_lora
    o2 = o1 + QK_ROPE
    o3 = o2 + 3 * n_b * HEAD
    o4 = o3 + 3 * n_c * HEAD
    w_kr = w_in[:, o1:o2]
    w_lat = jnp.concatenate([w_in[:, :o1], _pad_cols(w_kr, LANES), _pad_cols(_rot_cols(w_kr), LANES),
                             _pad_cols(w_in[:, o4:], LANES)], axis=1).astype(BF16)
    w_qkv = w_in[:, o2:o4].astype(BF16)
    qscale = jnp.concatenate([jnp.full((n_b * HEAD,), LOG2E * float(HEAD) ** -0.5, F32), jnp.ones((2 * n_b * HEAD,), F32),
                              jnp.full((n_c * HEAD,), LOG2E * float(HEAD) ** -0.5, F32), jnp.ones((2 * n_c * HEAD,), F32)])
    uq = w_uq.reshape(q_lora, n_a, QK_NOPE + QK_ROPE)
    zeros = jnp.zeros((q_lora, n_a, MLA_QK - QK_NOPE - QK_ROPE), F32)
    wq = jnp.concatenate([uq, zeros], axis=-1).reshape(q_lora, n_a * MLA_QK).astype(BF16)
    wqr = jnp.concatenate([_rot_cols(uq[..., QK_NOPE:]), jnp.zeros((q_lora, n_a, LANES - QK_ROPE), F32)],
                          axis=-1).reshape(q_lora, n_a * LANES).astype(BF16)
    ukv = w_ukv.reshape(kv_lora, n_a, QK_NOPE + HEAD)
    wkv = jnp.concatenate([ukv[..., :QK_NOPE].reshape(kv_lora, -1), ukv[..., QK_NOPE:].reshape(kv_lora, -1)],
                          axis=1).astype(BF16)
    return w_lat, w_qkv, qscale, wq, wqr, wkv


def _rope_tables(positions):
    inv = ROPE_THETA ** (-jnp.arange(0, QK_ROPE, 2, dtype=F32) / QK_ROPE)
    ang = positions.astype(F32).reshape(-1)[:, None] * inv
    pad = jnp.zeros((ang.shape[0], LANES - QK_ROPE), F32)
    cos, sin = jnp.cos(ang), jnp.sin(ang)
    return jnp.concatenate([cos, cos, pad], axis=1), jnp.concatenate([sin, sin, pad], axis=1)


def _mixer(h, mod, ctab, stab, g_norm1, w_in, g_q_lat, w_uq, g_kv_lat, w_ukv, rel_bias, b_forget,
           w_gate, b_gate, w_branch, w_out):
    b, s, d = h.shape
    t = b * s
    q_lora, kv_lora = g_q_lat.shape[0], g_kv_lat.shape[0]
    n_b, n_c = rel_bias.shape[0], b_forget.shape[0]
    n_a = w_ukv.shape[1] // (QK_NOPE + HEAD)
    w_lat, w_qkv, qscale, wq, wqr, wkv = _mixer_weights(w_in, w_uq, w_ukv, q_lora, kv_lora, n_a, n_b, n_c)

    u = _norm_mod(h, mod, g_norm1, row=0).reshape(t, d)
    n_lat = w_lat.shape[1]
    lat = _mm(u, w_lat, jnp.zeros((n_lat,), F32), jnp.ones((n_lat,), F32), F32)
    qkv = _mm(u, w_qkv, jnp.zeros((qscale.shape[0],), F32), qscale, BF16)
    gates = _mm(u, w_gate.astype(BF16), b_gate, jnp.ones_like(b_gate), BF16, act="sigmoid")

    q_cat, k_cat, v_a = _mla_prep(lat, ctab, stab, wq, wqr, wkv, g_q_lat, g_kv_lat, n_a)
    o_a = _attn(q_cat.reshape(b, s, -1), k_cat.reshape(b, s, -1), v_a.reshape(b, s, -1), 0, 0, 0, n_a, MLA_QK, "chunk")
    qkv3 = qkv.reshape(b, s, -1)
    o_b = _attn(qkv3, qkv3, qkv3, 0, n_b, 2 * n_b, n_b, HEAD, "band", _band_bias(rel_bias))
    f_off = q_lora + kv_lora + 2 * LANES
    f_t = lat[:, f_off:f_off + n_c].reshape(b, s, n_c).transpose(0, 2, 1).reshape(b * n_c, s)
    fcum = _fgate(f_t, jnp.tile(b_forget.astype(F32), b).reshape(b * n_c, 1))
    c0 = 3 * n_b
    o_c = _attn(qkv3, qkv3, qkv3, c0, c0 + n_c, c0 + 2 * n_c, n_c, HEAD, "decay", fcum)

    merged = _merge(o_a.reshape(t, -1), o_b.reshape(t, -1), o_c.reshape(t, -1), gates, w_branch.astype(BF16))
    return _out_proj(merged.reshape(b, s, d), w_out.astype(BF16), h, mod, row=2)


def kernel(x, c, positions, w_mod, b_mod, g_norm1, g_norm2, w_in, g_q_lat, w_uq, g_kv_lat, w_ukv, rel_bias, b_forget, w_gate, b_gate, w_branch, w_out, w_router, b_router, w_mlp1, b_mlp1, w_mlp2, b_mlp2, g_final):
    depth = w_mod.shape[0]
    ctab, stab = _rope_tables(positions)
    mods = _modulation(c, w_mod, b_mod)
    h, xs = x, None
    for l in range(depth):
        h = _mixer(h, mods[l], ctab, stab, g_norm1[l], w_in[l], g_q_lat[l], w_uq[l], g_kv_lat[l], w_ukv[l],
                   rel_bias[l], b_forget[l], w_gate[l], b_gate[l], w_branch[l], w_out[l])
        h, xs = _moe(h, mods[l], g_norm2[l], w_router[l], b_router[l], w_mlp1, b_mlp1, w_mlp2, b_mlp2, l, xs)
    return _final_norm(h, g_final)
```

```python
import functools

import jax
import jax.numpy as jnp
from jax import lax
from jax.experimental import pallas as pl
from jax.experimental.pallas import tpu as pltpu

CHUNK = 64
QK_NOPE = 128
QK_ROPE = 64
HEAD = 128
LEFT_CHUNKS = 8
MAX_LEFT = 128
ROPE_THETA = 10000.0
NEG_INF = -1e30
NORM_EPS = 1e-6
TOP_K = 4
SWIGLU_ALPHA = 1.702
SWIGLU_LIMIT = 7.0

LANES = 128
MLA_QK = 2 * LANES
VMEM_LIMIT = 56 * 1024 * 1024
MOE_ROWS = 256

ATTN_TILES = {"chunk": (1024, 512), "decay": (1024, 512), "band": (512, 256)}
BAND_LAG = 2
ONES_ROWS = 16
LOG2E = 1.4426950408889634

F32 = jnp.float32
BF16 = jnp.bfloat16


def _tile(n, pref):
    t = min(n, pref)
    while n % t:
        t //= 2
    return t


def _params(*sem):
    return pltpu.CompilerParams(dimension_semantics=sem, vmem_limit_bytes=VMEM_LIMIT)


def _dot(a, b):
    return jnp.dot(a, b, preferred_element_type=F32)


def _dot_nt(a, b):
    return lax.dot_general(a, b, (((1,), (1,)), ((), ())), preferred_element_type=F32)


def _sigmoid(x):
    return 1.0 / (1.0 + jnp.exp(-x))


def _mod_kernel(c_ref, w_ref, b_ref, o_ref):
    c = c_ref[...]
    ca = (c * _sigmoid(c)).astype(BF16)
    o_ref[0] = _dot(ca, w_ref[0].astype(BF16)) + b_ref[0]


def _modulation(c, w_mod, b_mod):
    nl, d, n = w_mod.shape
    b = c.shape[0]
    bp = 16
    cp = jnp.zeros((bp, d), F32).at[:b].set(c)
    tn = _tile(n, 1024)
    out = pl.pallas_call(
        _mod_kernel,
        grid=(nl, n // tn),
        in_specs=[pl.BlockSpec((bp, d), lambda l, j: (0, 0)),
                  pl.BlockSpec((1, d, tn), lambda l, j: (l, 0, j)),
                  pl.BlockSpec((1, 1, tn), lambda l, j: (l, 0, j))],
        out_specs=pl.BlockSpec((1, bp, tn), lambda l, j: (l, 0, j)),
        out_shape=jax.ShapeDtypeStruct((nl, bp, n), F32),
        compiler_params=_params("arbitrary", "arbitrary"),
    )(cp, w_mod, b_mod.reshape(nl, 1, n))
    return out[:, :b].reshape(nl, b, 6, d)


def _rms(x):
    return x * lax.rsqrt(jnp.mean(x * x, axis=-1, keepdims=True) + NORM_EPS)


def _norm_mod_kernel(h_ref, mod_ref, g_ref, u_ref, *, row):
    y = _rms(h_ref[0]) * g_ref[...]
    u = y * (1.0 + mod_ref[0, row + 1:row + 2, :]) + mod_ref[0, row:row + 1, :]
    u_ref[0] = u.astype(u_ref.dtype)


def _norm_mod(h, mod, g, row):
    b, s, d = h.shape
    ts = _tile(s, 512)
    return pl.pallas_call(
        functools.partial(_norm_mod_kernel, row=row),
        grid=(b, s // ts),
        in_specs=[pl.BlockSpec((1, ts, d), lambda i, j: (i, j, 0)),
                  pl.BlockSpec((1, 6, d), lambda i, j: (i, 0, 0)),
                  pl.BlockSpec((1, d), lambda i, j: (0, 0))],
        out_specs=pl.BlockSpec((1, ts, d), lambda i, j: (i, j, 0)),
        out_shape=jax.ShapeDtypeStruct((b, s, d), BF16),
        compiler_params=_params("arbitrary", "arbitrary"),
    )(h, mod, g.reshape(1, d))


def _final_norm_kernel(h_ref, g_ref, o_ref):
    o_ref[0] = _rms(h_ref[0]) * g_ref[...]


def _final_norm(h, g):
    b, s, d = h.shape
    ts = _tile(s, 512)
    return pl.pallas_call(
        _final_norm_kernel,
        grid=(b, s // ts),
        in_specs=[pl.BlockSpec((1, ts, d), lambda i, j: (i, j, 0)),
                  pl.BlockSpec((1, d), lambda i, j: (0, 0))],
        out_specs=pl.BlockSpec((1, ts, d), lambda i, j: (i, j, 0)),
        out_shape=jax.ShapeDtypeStruct((b, s, d), F32),
        compiler_params=_params("arbitrary", "arbitrary"),
    )(h, g.reshape(1, d))


def _mm_kernel(x_ref, w_ref, b_ref, s_ref, o_ref, *, act):
    acc = (_dot(x_ref[...], w_ref[...]) + b_ref[...]) * s_ref[...]
    if act == "sigmoid":
        acc = _sigmoid(acc)
    o_ref[...] = acc.astype(o_ref.dtype)


def _mm(x, w, bias, scale, out_dtype, act=None):
    m, k = x.shape
    n = w.shape[1]
    tm = _tile(m, 1024)
    tn = n if n <= 1536 else _tile(n, 1024)
    return pl.pallas_call(
        functools.partial(_mm_kernel, act=act),
        grid=(n // tn, m // tm),
        in_specs=[pl.BlockSpec((tm, k), lambda j, i: (i, 0)),
                  pl.BlockSpec((k, tn), lambda j, i: (0, j)),
                  pl.BlockSpec((1, tn), lambda j, i: (0, j)),
                  pl.BlockSpec((1, tn), lambda j, i: (0, j))],
        out_specs=pl.BlockSpec((tm, tn), lambda j, i: (i, j)),
        out_shape=jax.ShapeDtypeStruct((m, n), out_dtype),
        compiler_params=_params("arbitrary", "arbitrary"),
    )(x, w, bias.reshape(1, n), scale.reshape(1, n))


def _mla_prep_kernel(lat_ref, c_ref, s_ref, wq_ref, wqr_ref, wkv_ref, gq_ref, gkv_ref,
                     q_ref, k_ref, v_ref, *, q_lora, kv_lora, n_heads, scale):
    cos = c_ref[...]
    sin = s_ref[...]
    qn = (_rms(lat_ref[:, :q_lora]) * gq_ref[...]).astype(BF16)
    qa = _dot(qn, wq_ref[...])
    qr = _dot(qn, wqr_ref[...])
    kvn = (_rms(lat_ref[:, q_lora:q_lora + kv_lora]) * gkv_ref[...]).astype(BF16)
    kv = _dot(kvn, wkv_ref[...])
    o = q_lora + kv_lora
    k_rope = (lat_ref[:, o:o + LANES] * cos + lat_ref[:, o + LANES:o + 2 * LANES] * sin).astype(BF16)
    for h in range(n_heads):
        a = h * MLA_QK
        q_ref[:, a:a + LANES] = (qa[:, a:a + LANES] * scale).astype(BF16)
        rope = qa[:, a + LANES:a + MLA_QK] * cos + qr[:, h * LANES:(h + 1) * LANES] * sin
        q_ref[:, a + LANES:a + MLA_QK] = (rope * scale).astype(BF16)
        k_ref[:, a:a + LANES] = kv[:, h * LANES:(h + 1) * LANES].astype(BF16)
        k_ref[:, a + LANES:a + MLA_QK] = k_rope
    v_ref[...] = kv[:, n_heads * LANES:].astype(BF16)


def _mla_prep(lat, ctab, stab, wq, wqr, wkv, gq, gkv, n_heads):
    t, nl = lat.shape
    q_lora, kv_lora = gq.shape[0], gkv.shape[0]
    tm = _tile(t, 512)
    kern = functools.partial(_mla_prep_kernel, q_lora=q_lora, kv_lora=kv_lora, n_heads=n_heads,
                             scale=LOG2E * float(QK_NOPE + QK_ROPE) ** -0.5)
    full = lambda a: pl.BlockSpec(a.shape, lambda i: (0,) * a.ndim)
    gq2, gkv2 = gq.reshape(1, -1), gkv.reshape(1, -1)
    return pl.pallas_call(
        kern,
        grid=(t // tm,),
        in_specs=[pl.BlockSpec((tm, nl), lambda i: (i, 0)),
                  pl.BlockSpec((tm, LANES), lambda i: (i, 0)),
                  pl.BlockSpec((tm, LANES), lambda i: (i, 0)),
                  full(wq), full(wqr), full(wkv), full(gq2), full(gkv2)],
        out_specs=[pl.BlockSpec((tm, n_heads * MLA_QK), lambda i: (i, 0)),
                   pl.BlockSpec((tm, n_heads * MLA_QK), lambda i: (i, 0)),
                   pl.BlockSpec((tm, n_heads * HEAD), lambda i: (i, 0))],
        out_shape=[jax.ShapeDtypeStruct((t, n_heads * MLA_QK), BF16),
                   jax.ShapeDtypeStruct((t, n_heads * MLA_QK), BF16),
                   jax.ShapeDtypeStruct((t, n_heads * HEAD), BF16)],
        compiler_params=_params("arbitrary"),
    )(lat, ctab, stab, wq, wqr, wkv, gq2, gkv2)


def _fgate_kernel(f_ref, b_ref, o_ref):
    x = f_ref[...] + b_ref[...]
    v = jnp.minimum(x, 0.0) - jnp.log(1.0 + jnp.exp(-jnp.abs(x)))
    n = v.shape[1]
    lane = lax.broadcasted_iota(jnp.int32, v.shape, 1)
    sh = 1
    while sh < n:
        v = v + jnp.where(lane >= sh, pltpu.roll(v, sh, 1), 0.0)
        sh *= 2
    o_ref[...] = v * LOG2E


def _fgate(f_t, bias):
    r, s = f_t.shape
    return pl.pallas_call(
        _fgate_kernel,
        grid=(1,),
        in_specs=[pl.BlockSpec((r, s), lambda i: (0, 0)), pl.BlockSpec((r, 1), lambda i: (0, 0))],
        out_specs=pl.BlockSpec((r, s), lambda i: (0, 0)),
        out_shape=jax.ShapeDtypeStruct((r, s), F32),
        compiler_params=_params("arbitrary"),
    )(f_t, bias)


def _attn_kernel(*refs, tq, tk, mode, nbuf):
    if mode == "chunk":
        q_ref, k_ref, v_ref, o_ref, vt_ref, m_ref, acc_ref, *bufs = refs
        x_ref = f_ref = None
    elif mode == "decay":
        q_ref, k_ref, v_ref, x_ref, o_ref, vt_ref, m_ref, acc_ref, f_ref, *bufs = refs
    else:
        q_ref, k_ref, v_ref, x_ref, o_ref, vt_ref, m_ref, acc_ref, *bufs = refs
    sbuf, cbuf = bufs[:nbuf], bufs[nbuf:]
    i = pl.program_id(2)
    r = tq // tk

    @pl.when(i == 0)
    def _():
        vt_ref[0:HEAD, :] = v_ref[0].astype(F32).T.astype(BF16)
        vt_ref[HEAD:, :] = jnp.ones((ONES_ROWS, vt_ref.shape[1]), BF16)
        if mode == "decay":
            f_ref[...] = jnp.broadcast_to(x_ref[0], (LANES, x_ref.shape[2])).T

    q = q_ref[0]
    m_ref[...] = jnp.full(m_ref.shape, NEG_INF, F32)
    acc_ref[...] = jnp.zeros(acc_ref.shape, F32)

    def stage_a(j, slot, diag_off=None, bias_idx=None, penalty=None):
        start = pl.multiple_of(j * tk, tk)
        s = _dot_nt(k_ref[0, pl.ds(start, tk), :], q)
        if mode == "decay":
            s = s - jnp.tile(f_ref[pl.ds(start, tk), :], (1, tq // LANES))
        if mode == "band":
            s = s + x_ref[0, bias_idx]
            if penalty is not None:
                s = s + penalty
        elif diag_off is not None:
            row = lax.broadcasted_iota(jnp.int32, (tk, tq), 0) + diag_off
            col = lax.broadcasted_iota(jnp.int32, (tk, tq), 1)
            if mode == "chunk":
                row, col = row // CHUNK, col // CHUNK
            s = jnp.where(row <= col, s, NEG_INF)
        sbuf[slot][...] = s
        cbuf[slot][...] = jnp.max(s, axis=0, keepdims=True)

    def stage_b(j, slot):
        start = pl.multiple_of(j * tk, tk)
        m_prev = m_ref[...]
        m_new = jnp.maximum(m_prev, cbuf[slot][...])
        alpha = jnp.exp2(m_prev - m_new)
        p = jnp.exp2(sbuf[slot][...] - m_new).astype(BF16)
        acc_ref[...] = alpha * acc_ref[...] + _dot(vt_ref[:, pl.ds(start, tk)], p)
        m_ref[...] = m_new

    if mode == "band":
        left = nbuf - r
        first = r * i - left
        blk = [jnp.maximum(first + d, 0) for d in range(nbuf)]
        for d in range(nbuf + BAND_LAG):
            if d < nbuf:
                penalty = jnp.where(first + d < 0, NEG_INF, 0.0).astype(F32) if d < left else None
                stage_a(blk[d], d, bias_idx=d, penalty=penalty)
            if d >= BAND_LAG:
                stage_b(blk[d - BAND_LAG], d - BAND_LAG)
    else:
        stage_a(r * i, 0, diag_off=0)
        stage_a(r * i + 1, 1, diag_off=tk)
        stage_b(r * i, 0)

        def body(p, carry):
            stage_a(2 * p, 0)
            stage_b(jnp.where(p == 0, r * i + 1, 2 * p - 1), 1)
            stage_a(2 * p + 1, 1)
            stage_b(2 * p, 0)
            return carry

        lax.fori_loop(0, i, body, 0)
        stage_b(jnp.where(i == 0, 1, r * i - 1), 1)
    acc = acc_ref[...]
    o_ref[0] = (acc[:HEAD] / acc[HEAD:HEAD + 1]).T.astype(o_ref.dtype)


def _band_bias(rel_bias):
    tq, tk = ATTN_TILES["band"]
    left = LEFT_CHUNKS * CHUNK // tk
    period = tk + tq
    x = jnp.arange(period)
    x = jnp.where(x < tq, x, x - period)
    qpos = jnp.arange(tq)[None, :]
    tabs = []
    for d in range(tq // tk + left):
        rel = (d - left) * tk - x
        vec = rel_bias[:, jnp.clip(rel, -MAX_LEFT, CHUNK - 1) + MAX_LEFT].astype(F32) * LOG2E
        skew = jnp.broadcast_to(vec[:, None, :], (vec.shape[0], tk, period)).reshape(vec.shape[0], tk * period)
        bias = skew[:, :tk * (period - 1)].reshape(vec.shape[0], tk, period - 1)[:, :, :tq]
        kpos = jnp.arange(tk)[:, None] + (d - left) * tk
        dchunk = kpos // CHUNK - qpos // CHUNK
        tabs.append(jnp.where((dchunk <= 0) & (dchunk >= -LEFT_CHUNKS), bias, NEG_INF))
    return jnp.stack(tabs, axis=1)


def _attn(q_arr, k_arr, v_arr, q_off, k_off, v_off, n_heads, dk, mode, extra=None):
    b, s, _ = q_arr.shape
    tq, tk = ATTN_TILES[mode]
    assert s % tq == 0 and tq == 2 * tk
    nbuf = 2
    in_specs = [pl.BlockSpec((1, tq, dk), lambda bi, h, i: (bi, i, q_off + h)),
                pl.BlockSpec((1, s, dk), lambda bi, h, i: (bi, 0, k_off + h)),
                pl.BlockSpec((1, s, HEAD), lambda bi, h, i: (bi, 0, v_off + h))]
    args = [q_arr, k_arr, v_arr]
    scratch = [pltpu.VMEM((HEAD + ONES_ROWS, s), BF16), pltpu.VMEM((1, tq), F32),
               pltpu.VMEM((HEAD + ONES_ROWS, tq), F32)]
    if mode == "decay":
        in_specs.append(pl.BlockSpec((1, 1, s), lambda bi, h, i: (bi * n_heads + h, 0, 0)))
        args.append(extra.reshape(b * n_heads, 1, s))
        scratch.append(pltpu.VMEM((s, LANES), F32))
    if mode == "band":
        nbuf = extra.shape[1]
        in_specs.append(pl.BlockSpec((1, nbuf, tk, tq), lambda bi, h, i: (h, 0, 0, 0)))
        args.append(extra)
    scratch += [pltpu.VMEM((tk, tq), F32)] * nbuf + [pltpu.VMEM((1, tq), F32)] * nbuf
    return pl.pallas_call(
        functools.partial(_attn_kernel, tq=tq, tk=tk, mode=mode, nbuf=nbuf),
        grid=(b, n_heads, s // tq),
        in_specs=in_specs,
        out_specs=pl.BlockSpec((1, tq, HEAD), lambda bi, h, i: (bi, i, h)),
        out_shape=jax.ShapeDtypeStruct((b, s, n_heads * HEAD), BF16),
        scratch_shapes=scratch,
        compiler_params=_params("arbitrary", "arbitrary", "arbitrary"),
    )(*args)


def _merge_kernel(oa_ref, ob_ref, oc_ref, ga_ref, gb_ref, gc_ref, wa_ref, wb_ref, wc_ref, o_ref):
    acc = ga_ref[...].astype(F32) * _dot(oa_ref[...], wa_ref[0])
    acc += gb_ref[...].astype(F32) * _dot(ob_ref[...], wb_ref[0])
    acc += gc_ref[...].astype(F32) * _dot(oc_ref[...], wc_ref[0])
    o_ref[...] = acc.astype(o_ref.dtype)


def _merge(oa, ob, oc, gates, w_branch):
    t, bw = oa.shape
    d = w_branch.shape[2]
    tm, tn = _tile(t, 512), _tile(d, 1024)
    nj = d // tn
    o_spec = pl.BlockSpec((tm, bw), lambda j, i: (i, 0))
    g_spec = lambda g: pl.BlockSpec((tm, tn), lambda j, i: (i, g * nj + j))
    w_spec = lambda g: pl.BlockSpec((1, bw, tn), lambda j, i: (g, 0, j))
    return pl.pallas_call(
        _merge_kernel,
        grid=(nj, t // tm),
        in_specs=[o_spec, o_spec, o_spec, g_spec(0), g_spec(1), g_spec(2), w_spec(0), w_spec(1), w_spec(2)],
        out_specs=pl.BlockSpec((tm, tn), lambda j, i: (i, j)),
        out_shape=jax.ShapeDtypeStruct((t, d), BF16),
        compiler_params=_params("arbitrary", "arbitrary"),
    )(oa, ob, oc, gates, gates, gates, w_branch, w_branch, w_branch)


def _out_proj_kernel(x_ref, w_ref, h_ref, mod_ref, o_ref, *, row):
    o_ref[0] = h_ref[0] + mod_ref[0, row:row + 1, :] * _dot(x_ref[0], w_ref[...])


def _out_proj(x, w, h, mod, row):
    b, s, k = x.shape
    d = w.shape[1]
    tm, tn = _tile(s, 1024), _tile(d, 1024)
    return pl.pallas_call(
        functools.partial(_out_proj_kernel, row=row),
        grid=(d // tn, b, s // tm),
        in_specs=[pl.BlockSpec((1, tm, k), lambda j, bi, i: (bi, i, 0)),
                  pl.BlockSpec((k, tn), lambda j, bi, i: (0, j)),
                  pl.BlockSpec((1, tm, tn), lambda j, bi, i: (bi, i, j)),
                  pl.BlockSpec((1, 6, tn), lambda j, bi, i: (bi, 0, j))],
        out_specs=pl.BlockSpec((1, tm, tn), lambda j, bi, i: (bi, i, j)),
        out_shape=jax.ShapeDtypeStruct((b, s, d), F32),
        compiler_params=_params("arbitrary", "arbitrary", "arbitrary"),
    )(x, w, h, mod)


def _router_kernel(h_ref, mod_ref, g_ref, wr_ref, br_ref, u_ref, idx_ref, wt_ref, rank_ref, cnt_ref, carry, *, row):
    first = jnp.logical_and(pl.program_id(0) == 0, pl.program_id(1) == 0)

    @pl.when(first)
    def _():
        carry[...] = jnp.zeros(carry.shape, F32)

    y = _rms(h_ref[0]) * g_ref[...]
    u = y * (1.0 + mod_ref[0, row + 1:row + 2, :]) + mod_ref[0, row:row + 1, :]
    u_ref[0] = u
    logits = jnp.dot(u, wr_ref[...], preferred_element_type=F32, precision=lax.Precision.HIGHEST) + br_ref[...]
    ts, ne = logits.shape
    lane = lax.broadcasted_iota(jnp.int32, (ts, ne), 1)
    out_lane = lax.broadcasted_iota(jnp.int32, (ts, TOP_K), 1)
    idx = jnp.zeros((ts, TOP_K), jnp.int32)
    val = jnp.zeros((ts, TOP_K), F32)
    onehots = []
    for kk in range(TOP_K):
        m = jnp.max(logits, axis=-1, keepdims=True)
        pick = jnp.min(jnp.where(logits == m, lane, ne), axis=-1, keepdims=True)
        idx = jnp.where(out_lane == kk, pick, idx)
        val = jnp.where(out_lane == kk, m, val)
        onehots.append(lane == pick)
        logits = jnp.where(lane == pick, -jnp.inf, logits)
    e = jnp.exp(val - jnp.max(val, axis=-1, keepdims=True))
    idx_ref[0] = idx
    wt_ref[0] = e / jnp.sum(e, axis=-1, keepdims=True)

    hits = jnp.zeros((ts, ne), F32)
    for oh in onehots:
        hits = hits + oh.astype(F32)
    earlier = lax.broadcasted_iota(jnp.int32, (ts, ts), 1) < lax.broadcasted_iota(jnp.int32, (ts, ts), 0)
    before = _dot(earlier.astype(F32).astype(BF16), hits.astype(BF16)) + carry[...]
    rank = jnp.zeros((ts, TOP_K), F32)
    for kk, oh in enumerate(onehots):
        rank = jnp.where(out_lane == kk, jnp.sum(jnp.where(oh, before, 0.0), axis=-1, keepdims=True), rank)
    rank_ref[0] = rank.astype(jnp.int32)
    carry[...] = carry[...] + jnp.sum(hits, axis=0, keepdims=True)
    cnt_ref[...] = carry[...].astype(jnp.int32)


def _router(h, mod, g, w_router, b_router, row):
    b, s, d = h.shape
    ne = w_router.shape[1]
    ts = _tile(s, 512)
    small = lambda: pl.BlockSpec((1, ts, TOP_K), lambda i, j: (i, j, 0))
    return pl.pallas_call(
        functools.partial(_router_kernel, row=row),
        grid=(b, s // ts),
        in_specs=[pl.BlockSpec((1, ts, d), lambda i, j: (i, j, 0)),
                  pl.BlockSpec((1, 6, d), lambda i, j: (i, 0, 0)),
                  pl.BlockSpec((1, d), lambda i, j: (0, 0)),
                  pl.BlockSpec((d, ne), lambda i, j: (0, 0)),
                  pl.BlockSpec((1, ne), lambda i, j: (0, 0))],
        out_specs=[pl.BlockSpec((1, ts, d), lambda i, j: (i, j, 0)), small(), small(), small(),
                   pl.BlockSpec((1, ne), lambda i, j: (0, 0))],
        out_shape=[jax.ShapeDtypeStruct((b, s, d), F32),
                   jax.ShapeDtypeStruct((b, s, TOP_K), jnp.int32),
                   jax.ShapeDtypeStruct((b, s, TOP_K), F32),
                   jax.ShapeDtypeStruct((b, s, TOP_K), jnp.int32),
                   jax.ShapeDtypeStruct((1, ne), jnp.int32)],
        scratch_shapes=[pltpu.VMEM((1, ne), F32)],
        compiler_params=_params("arbitrary", "arbitrary"),
    )(h, mod, g.reshape(1, d), w_router, b_router.reshape(1, ne))


DMA_PRIORITIES = 2


def _dispatch_kernel(dest_ref, u_ref, xs_in, xs_out, sem, *, n):
    del xs_in

    def copy(r, dst_row):
        return pltpu.make_async_copy(u_ref.at[r // TOP_K], xs_out.at[dst_row], sem)

    for r in range(n):
        copy(r, dest_ref[0, 0, r]).start(priority=r % DMA_PRIORITIES)
    for r in range(n):
        copy(r, 0).wait()


def _dispatch(u, dest, xs_init):
    t, d = u.shape
    n_rows = xs_init.shape[0]
    tc = _tile(t, 128)
    nt = t // tc
    n = TOP_K * tc
    return pl.pallas_call(
        functools.partial(_dispatch_kernel, n=n),
        grid=(nt,),
        in_specs=[pl.BlockSpec((1, 1, n), lambda i: (i, 0, 0), memory_space=pltpu.SMEM),
                  pl.BlockSpec((tc, d), lambda i: (i, 0)),
                  pl.BlockSpec(memory_space=pl.ANY)],
        out_specs=pl.BlockSpec(memory_space=pl.ANY),
        out_shape=jax.ShapeDtypeStruct((n_rows, d), F32),
        scratch_shapes=[pltpu.SemaphoreType.DMA(())],
        input_output_aliases={2: 0},
        compiler_params=_params("arbitrary"),
    )(dest.reshape(nt, 1, n), u, xs_init)


def _group_start(be_ref, nu_ref):
    i = pl.program_id(1)
    prev = be_ref[jnp.maximum(i - 1, 0)]
    return jnp.logical_and(i < nu_ref[0], jnp.logical_or(i == 0, be_ref[i] != prev))


def _stage_weights(be_ref, nx_ref, nu_ref, fetch, landing, staged):
    i = pl.program_id(1)

    @pl.when(_group_start(be_ref, nu_ref))
    def _():
        @pl.when(i == 0)
        def _():
            for c in fetch(be_ref[i]):
                c.start()

        for c in fetch(be_ref[i]):
            c.wait()
        for src, dst in zip(landing, staged):
            dst[...] = src[...].astype(BF16)

        @pl.when(nx_ref[i] >= 0)
        def _():
            for c in fetch(nx_ref[i]):
                c.start()


def _expert_up_kernel(be_ref, nx_ref, nu_ref, x_ref, w_hbm, bg_ref, bl_ref, o_ref, wg_f, wl_f, wg_bf, wl_bf, sem,
                      *, layer, tn, ff):
    i = pl.program_id(1)
    col = pl.multiple_of(pl.program_id(0) * tn, tn)

    def fetch(e):
        return [pltpu.make_async_copy(w_hbm.at[layer, e, :, pl.ds(col, tn)], wg_f, sem.at[0]),
                pltpu.make_async_copy(w_hbm.at[layer, e, :, pl.ds(ff + col, tn)], wl_f, sem.at[1])]

    _stage_weights(be_ref, nx_ref, nu_ref, fetch, (wg_f, wl_f), (wg_bf, wl_bf))

    @pl.when(i < nu_ref[0])
    def _():
        x = x_ref[...].astype(BF16)
        glu = jnp.minimum(_dot(x, wg_bf[...]) + bg_ref[0, 0], SWIGLU_LIMIT)
        lin = jnp.clip(_dot(x, wl_bf[...]) + bl_ref[0, 0], -SWIGLU_LIMIT, SWIGLU_LIMIT)
        o_ref[...] = (glu * _sigmoid(SWIGLU_ALPHA * glu) * (lin + 1.0)).astype(o_ref.dtype)

    @pl.when(i >= nu_ref[0])
    def _():
        o_ref[...] = jnp.zeros(o_ref.shape, o_ref.dtype)


def _expert_up(xs, w1, b1, layer, block_e, next_e, n_used):
    n_rows, d = xs.shape
    nl, ne, _, ff2 = w1.shape
    ff = ff2 // 2
    tn = _tile(ff, 1024)
    nj = ff // tn
    nb = n_rows // MOE_ROWS
    grid_spec = pltpu.PrefetchScalarGridSpec(
        num_scalar_prefetch=3,
        grid=(nj, nb),
        in_specs=[pl.BlockSpec((MOE_ROWS, d), lambda j, i, be, nx, nu: (i, 0)),
                  pl.BlockSpec(memory_space=pl.ANY),
                  pl.BlockSpec((1, 1, 1, tn), lambda j, i, be, nx, nu: (layer, be[i], 0, j)),
                  pl.BlockSpec((1, 1, 1, tn), lambda j, i, be, nx, nu: (layer, be[i], 0, nj + j))],
        out_specs=pl.BlockSpec((MOE_ROWS, tn), lambda j, i, be, nx, nu: (i, j)),
        scratch_shapes=[pltpu.VMEM((d, tn), F32), pltpu.VMEM((d, tn), F32),
                        pltpu.VMEM((d, tn), BF16), pltpu.VMEM((d, tn), BF16), pltpu.SemaphoreType.DMA((2,))])
    b1r = b1.reshape(nl, ne, 1, ff2)
    return pl.pallas_call(
        functools.partial(_expert_up_kernel, layer=layer, tn=tn, ff=ff),
        grid_spec=grid_spec,
        out_shape=jax.ShapeDtypeStruct((n_rows, ff), BF16),
        compiler_params=_params("arbitrary", "arbitrary"),
    )(block_e, next_e, n_used, xs, w1, b1r, b1r)


def _expert_down_kernel(be_ref, nx_ref, nu_ref, a_ref, w_hbm, b_ref, o_ref, w_f, w_bf, sem, *, layer, tn):
    i = pl.program_id(1)
    col = pl.multiple_of(pl.program_id(0) * tn, tn)

    def fetch(e):
        return [pltpu.make_async_copy(w_hbm.at[layer, e, :, pl.ds(col, tn)], w_f, sem.at[0])]

    _stage_weights(be_ref, nx_ref, nu_ref, fetch, (w_f,), (w_bf,))

    @pl.when(i < nu_ref[0])
    def _():
        o_ref[...] = _dot(a_ref[...], w_bf[...]) + b_ref[0, 0]

    @pl.when(i >= nu_ref[0])
    def _():
        o_ref[...] = jnp.zeros(o_ref.shape, o_ref.dtype)


def _expert_down(act, w2, b2, layer, block_e, next_e, n_used):
    n_rows, ff = act.shape
    nl, ne, _, d = w2.shape
    tn = _tile(d, 2048)
    nb = n_rows // MOE_ROWS
    grid_spec = pltpu.PrefetchScalarGridSpec(
        num_scalar_prefetch=3,
        grid=(d // tn, nb),
        in_specs=[pl.BlockSpec((MOE_ROWS, ff), lambda j, i, be, nx, nu: (i, 0)),
                  pl.BlockSpec(memory_space=pl.ANY),
                  pl.BlockSpec((1, 1, 1, tn), lambda j, i, be, nx, nu: (layer, be[i], 0, j))],
        out_specs=pl.BlockSpec((MOE_ROWS, tn), lambda j, i, be, nx, nu: (i, j)),
        scratch_shapes=[pltpu.VMEM((ff, tn), F32), pltpu.VMEM((ff, tn), BF16), pltpu.SemaphoreType.DMA((1,))])
    return pl.pallas_call(
        functools.partial(_expert_down_kernel, layer=layer, tn=tn),
        grid_spec=grid_spec,
        out_shape=jax.ShapeDtypeStruct((n_rows, d), F32),
        compiler_params=_params("arbitrary", "arbitrary"),
    )(block_e, next_e, n_used, act, w2, b2.reshape(nl, ne, 1, d))


def _combine_kernel(pos_ref, nxt_ref, y_hbm, wt_ref, h_ref, mod_ref, o_ref, buf, sem, *, tc, row, nt):
    n = TOP_K * tc
    i = pl.program_id(0)
    slot = lax.rem(i, 2)

    def copy(r, src_row, s):
        return pltpu.make_async_copy(y_hbm.at[src_row], buf.at[s, r], sem.at[s])

    @pl.when(i == 0)
    def _():
        for r in range(n):
            copy(r, pos_ref[0, 0, r], 0).start(priority=r % DMA_PRIORITIES)

    @pl.when(i + 1 < nt)
    def _():
        for r in range(n):
            copy(r, nxt_ref[0, 0, r], 1 - slot).start(priority=r % DMA_PRIORITIES)

    for r in range(n):
        copy(r, 0, slot).wait()
    wt = wt_ref[...]
    rows = buf.at[slot]
    tot = wt[:, 0:1] * rows[0:tc, :]
    for kk in range(1, TOP_K):
        tot = tot + wt[:, kk:kk + 1] * rows[kk * tc:(kk + 1) * tc, :]
    o_ref[0] = h_ref[0] + mod_ref[0, row:row + 1, :] * tot


def _combine(ys, pos, wts, h, mod, row):
    b, s, d = h.shape
    tc = _tile(s, 128)
    nt = (b * s) // tc
    spb = s // tc
    pos_blk = pos.reshape(nt, tc, TOP_K).transpose(0, 2, 1).reshape(nt, 1, TOP_K * tc)
    return pl.pallas_call(
        functools.partial(_combine_kernel, tc=tc, row=row, nt=nt),
        grid=(nt,),
        in_specs=[pl.BlockSpec((1, 1, TOP_K * tc), lambda i: (i, 0, 0), memory_space=pltpu.SMEM),
                  pl.BlockSpec((1, 1, TOP_K * tc), lambda i: (jnp.minimum(i + 1, nt - 1), 0, 0),
                               memory_space=pltpu.SMEM),
                  pl.BlockSpec(memory_space=pl.ANY),
                  pl.BlockSpec((tc, TOP_K), lambda i: (i, 0)),
                  pl.BlockSpec((1, tc, d), lambda i: (i // spb, i % spb, 0)),
                  pl.BlockSpec((1, 6, d), lambda i: (i // spb, 0, 0))],
        out_specs=pl.BlockSpec((1, tc, d), lambda i: (i // spb, i % spb, 0)),
        out_shape=jax.ShapeDtypeStruct((b, s, d), F32),
        scratch_shapes=[pltpu.VMEM((2, TOP_K * tc, d), F32), pltpu.SemaphoreType.DMA((2,))],
        compiler_params=_params("arbitrary"),
    )(pos_blk, pos_blk, ys, wts, h, mod)


def _routing_tables(top_idx, rank, counts, n_experts):
    padded = (counts + MOE_ROWS - 1) // MOE_ROWS * MOE_ROWS
    pend = jnp.cumsum(padded)
    pstart = pend - padded
    experts = jnp.arange(n_experts, dtype=jnp.int32)
    dest = rank + jnp.sum(jnp.where(top_idx[..., None] == experts, pstart, 0), axis=-1)
    n_rows = top_idx.size + n_experts * MOE_ROWS
    nb = n_rows // MOE_ROWS
    starts = jnp.arange(nb, dtype=jnp.int32) * MOE_ROWS
    block_e = jnp.minimum(jnp.sum((pend[None, :] <= starts[:, None]).astype(jnp.int32), axis=1), n_experts - 1)
    n_used = (pend[-1:] // MOE_ROWS).astype(jnp.int32)
    blocks = jnp.arange(nb, dtype=jnp.int32)
    prev_e = jnp.concatenate([block_e[:1] - 1, block_e[:-1]])
    first_of_group = jnp.where((block_e != prev_e) & (blocks < n_used[0]), blocks, nb)
    later = jnp.concatenate([lax.cummin(first_of_group, reverse=True)[1:], jnp.full((1,), nb, jnp.int32)])
    next_e = jnp.where(later < nb, block_e[jnp.minimum(later, nb - 1)], -1)
    return dest.astype(jnp.int32), block_e.astype(jnp.int32), next_e.astype(jnp.int32), n_used, n_rows


def _moe(h, mod, g, w_router, b_router, w1, b1, w2, b2, layer, xs_prev):
    b, s, d = h.shape
    t = b * s
    ne = w_router.shape[1]
    u, top_idx, top_w, rank, counts = _router(h, mod, g, w_router, b_router, row=3)
    dest, block_e, next_e, n_used, n_rows = _routing_tables(top_idx.reshape(t, TOP_K), rank.reshape(t, TOP_K),
                                                            counts.reshape(ne), ne)
    xs = _dispatch(u.reshape(t, d), dest, jnp.zeros((n_rows, d), F32) if xs_prev is None else xs_prev)
    act = _expert_up(xs, w1, b1, layer, block_e, next_e, n_used)
    ys = _expert_down(act, w2, b2, layer, block_e, next_e, n_used)
    return _combine(ys, dest, top_w.reshape(t, TOP_K), h, mod, row=5), xs


def _rot_cols(w):
    half = w.shape[-1] // 2
    return jnp.concatenate([-w[..., half:], w[..., :half]], axis=-1)


def _pad_cols(w, width):
    return jnp.pad(w, ((0, 0), (0, width - w.shape[1])))


def _mixer_weights(w_in, w_uq, w_ukv, q_lora, kv_lora, n_a, n_b, n_c):
    d = w_in.shape[0]
    o1 = q_lora + kv_lora
    o2 = o1 + QK_ROPE
    o3 = o2 + 3 * n_b * HEAD
    o4 = o3 + 3 * n_c * HEAD
    w_kr = w_in[:, o1:o2]
    w_lat = jnp.concatenate([w_in[:, :o1], _pad_cols(w_kr, LANES), _pad_cols(_rot_cols(w_kr), LANES),
                             _pad_cols(w_in[:, o4:], LANES)], axis=1).astype(BF16)
    w_qkv = w_in[:, o2:o4].astype(BF16)
    qscale = jnp.concatenate([jnp.full((n_b * HEAD,), LOG2E * float(HEAD) ** -0.5, F32), jnp.ones((2 * n_b * HEAD,), F32),
                              jnp.full((n_c * HEAD,), LOG2E * float(HEAD) ** -0.5, F32), jnp.ones((2 * n_c * HEAD,), F32)])
    uq = w_uq.reshape(q_lora, n_a, QK_NOPE + QK_ROPE)
    zeros = jnp.zeros((q_lora, n_a, MLA_QK - QK_NOPE - QK_ROPE), F32)
    wq = jnp.concatenate([uq, zeros], axis=-1).reshape(q_lora, n_a * MLA_QK).astype(BF16)
    wqr = jnp.concatenate([_rot_cols(uq[..., QK_NOPE:]), jnp.zeros((q_lora, n_a, LANES - QK_ROPE), F32)],
                          axis=-1).reshape(q_lora, n_a * LANES).astype(BF16)
    ukv = w_ukv.reshape(kv_lora, n_a, QK_NOPE + HEAD)
    wkv = jnp.concatenate([ukv[..., :QK_NOPE].reshape(kv_lora, -1), ukv[..., QK_NOPE:].reshape(kv_lora, -1)],
                          axis=1).astype(BF16)
    return w_lat, w_qkv, qscale, wq, wqr, wkv


def _rope_tables(positions):
    inv = ROPE_THETA ** (-jnp.arange(0, QK_ROPE, 2, dtype=F32) / QK_ROPE)
    ang = positions.astype(F32).reshape(-1)[:, None] * inv
    pad = jnp.zeros((ang.shape[0], LANES - QK_ROPE), F32)
    cos, sin = jnp.cos(ang), jnp.sin(ang)
    return jnp.concatenate([cos, cos, pad], axis=1), jnp.concatenate([sin, sin, pad], axis=1)


def _mixer(h, mod, ctab, stab, g_norm1, w_in, g_q_lat, w_uq, g_kv_lat, w_ukv, rel_bias, b_forget,
           w_gate, b_gate, w_branch, w_out):
    b, s, d = h.shape
    t = b * s
    q_lora, kv_lora = g_q_lat.shape[0], g_kv_lat.shape[0]
    n_b, n_c = rel_bias.shape[0], b_forget.shape[0]
    n_a = w_ukv.shape[1] // (QK_NOPE + HEAD)
    w_lat, w_qkv, qscale, wq, wqr, wkv = _mixer_weights(w_in, w_uq, w_ukv, q_lora, kv_lora, n_a, n_b, n_c)

    u = _norm_mod(h, mod, g_norm1, row=0).reshape(t, d)
    n_lat = w_lat.shape[1]
    lat = _mm(u, w_lat, jnp.zeros((n_lat,), F32), jnp.ones((n_lat,), F32), F32)
    qkv = _mm(u, w_qkv, jnp.zeros((qscale.shape[0],), F32), qscale, BF16)
    gates = _mm(u, w_gate.astype(BF16), b_gate, jnp.ones_like(b_gate), BF16, act="sigmoid")

    q_cat, k_cat, v_a = _mla_prep(lat, ctab, stab, wq, wqr, wkv, g_q_lat, g_kv_lat, n_a)
    o_a = _attn(q_cat.reshape(b, s, -1), k_cat.reshape(b, s, -1), v_a.reshape(b, s, -1), 0, 0, 0, n_a, MLA_QK, "chunk")
    qkv3 = qkv.reshape(b, s, -1)
    o_b = _attn(qkv3, qkv3, qkv3, 0, n_b, 2 * n_b, n_b, HEAD, "band", _band_bias(rel_bias))
    f_off = q_lora + kv_lora + 2 * LANES
    f_t = lat[:, f_off:f_off + n_c].reshape(b, s, n_c).transpose(0, 2, 1).reshape(b * n_c, s)
    fcum = _fgate(f_t, jnp.tile(b_forget.astype(F32), b).reshape(b * n_c, 1))
    c0 = 3 * n_b
    o_c = _attn(qkv3, qkv3, qkv3, c0, c0 + n_c, c0 + 2 * n_c, n_c, HEAD, "decay", fcum)

    merged = _merge(o_a.reshape(t, -1), o_b.reshape(t, -1), o_c.reshape(t, -1), gates, w_branch.astype(BF16))
    return _out_proj(merged.reshape(b, s, d), w_out.astype(BF16), h, mod, row=2)


def kernel(x, c, positions, w_mod, b_mod, g_norm1, g_norm2, w_in, g_q_lat, w_uq, g_kv_lat, w_ukv, rel_bias, b_forget, w_gate, b_gate, w_branch, w_out, w_router, b_router, w_mlp1, b_mlp1, w_mlp2, b_mlp2, g_final):
    depth = w_mod.shape[0]
    ctab, stab = _rope_tables(positions)
    mods = _modulation(c, w_mod, b_mod)
    h, xs = x, None
    for l in range(depth):
        h = _mixer(h, mods[l], ctab, stab, g_norm1[l], w_in[l], g_q_lat[l], w_uq[l], g_kv_lat[l], w_ukv[l],
                   rel_bias[l], b_forget[l], w_gate[l], b_gate[l], w_branch[l], w_out[l])
        h, xs = _moe(h, mods[l], g_norm2[l], w_router[l], b_router[l], w_mlp1, b_mlp1, w_mlp2, b_mlp2, l, xs)
    return _final_norm(h, g_final)
```
